```python
import jax, jax.numpy as jnp
from jax import lax
import numpy as np

D_MODEL = 2048
BATCH = 2
SEQ = 8192
DEPTH = 4

CHUNK = 64
N_MIXERS = 2
POOL_WINDOWS = (2, 4, 8, 16)
N_POOL_GROUPS = len(POOL_WINDOWS)
POOL_GROUP_DIM = D_MODEL // N_POOL_GROUPS
HGRN_EXPAND = 128
HGRN_HEADS = D_MODEL // HGRN_EXPAND
HGRN_DK = HGRN_EXPAND
HGRN_DV = D_MODEL // HGRN_HEADS
D_FF = -(-8 * D_MODEL // (3 * 256)) * 256
N_MOD = 6
EPS = 1e-6

kernel_name = 'hybrid_pool_hgrn2_adaln_trunk'


def rms_norm(x, gain):
    xf = x.astype(jnp.float32)
    y = xf * lax.rsqrt(jnp.mean(xf * xf, axis=-1, keepdims=True) + EPS)
    return (y * gain.astype(jnp.float32)).astype(x.dtype)


def pool_mixer(h, w_group, chan_scale):
    B, S, D = h.shape
    hg = h.astype(jnp.float32).reshape(B, S, N_POOL_GROUPS, POOL_GROUP_DIM)
    cs = jnp.cumsum(hg, axis=1)
    pos = jnp.arange(1, S + 1, dtype=jnp.float32)
    outs = []
    for g, w in enumerate(POOL_WINDOWS):
        csg = cs[:, :, g]
        lo = jnp.concatenate([jnp.zeros((B, w, POOL_GROUP_DIM), jnp.float32), csg[:, :S - w]], axis=1)
        mean = (csg - lo) / jnp.minimum(pos, float(w))[None, :, None]
        outs.append(mean - hg[:, :, g])
    d = jnp.stack(outs, axis=2).astype(h.dtype)
    y = jnp.einsum('bsgc,gcd->bsgd', d, w_group).reshape(B, S, D)
    return y * chan_scale


def hgrn_lower_bounds(lb_params):
    p = jax.nn.softmax(lb_params.astype(jnp.float32), axis=0)
    return jnp.cumsum(p, axis=0) - p[0]


def hgrn2_mixer(h, w_in, w_out, out_norm_gain, lower_bound):
    B, S, D = h.shape
    nc = S // CHUNK
    proj = h @ w_in
    q, f_pre, v, g_out = jnp.split(proj, 4, axis=-1)
    f = lower_bound + (1.0 - lower_bound) * jax.nn.sigmoid(f_pre.astype(jnp.float32))
    log_f = jnp.log(f)
    k = 1.0 - f

    def to_chunks(t, dh):
        return t.astype(jnp.float32).reshape(B, nc, CHUNK, HGRN_HEADS, dh).transpose(1, 0, 3, 2, 4)

    qc = to_chunks(q, HGRN_DK)
    kc = to_chunks(k, HGRN_DK)
    vc = to_chunks(v, HGRN_DV)
    bc = jnp.cumsum(to_chunks(log_f, HGRN_DK), axis=3)
    mask = jnp.tril(jnp.ones((CHUNK, CHUNK), dtype=bool))

    def step(state, xs):
        q_c, k_c, v_c, b_c = xs
        diff = b_c[:, :, :, None, :] - b_c[:, :, None, :, :]
        decay = jnp.exp(jnp.where(mask[:, :, None], diff, -jnp.inf))
        scores = jnp.einsum('bhtd,bhsd,bhtsd->bhts', q_c, k_c, decay)
        o = (jnp.einsum('bhts,bhsv->bhtv', scores, v_c)
             + jnp.einsum('bhtd,bhdv->bhtv', q_c * jnp.exp(b_c), state))
        b_end = b_c[:, :, -1, :]
        state = (state * jnp.exp(b_end)[..., None]
                 + jnp.einsum('bhsd,bhsv->bhdv', k_c * jnp.exp(b_end[:, :, None, :] - b_c), v_c))
        return state, o

    state0 = jnp.zeros((B, HGRN_HEADS, HGRN_DK, HGRN_DV), jnp.float32)
    _, o = lax.scan(step, state0, (qc, kc, vc, bc))
    o = o.transpose(1, 0, 3, 2, 4).reshape(B, S, HGRN_HEADS, HGRN_DV)
    o = o * lax.rsqrt(jnp.mean(o * o, axis=-1, keepdims=True) + EPS)
    o = o.reshape(B, S, D) * out_norm_gain.astype(jnp.float32) * jax.nn.silu(g_out.astype(jnp.float32))
    return o.astype(h.dtype) @ w_out


def swiglu(h, w_in, w_out):
    a, b = jnp.split(h @ w_in, 2, axis=-1)
    return (jax.nn.silu(a) * b) @ w_out


def setup_inputs(seed: int = 0) -> dict:
    key = jax.random.key(seed)
    ks = jax.random.split(key, 16)
    n_pool = (DEPTH + 1) // 2
    n_hgrn = DEPTH // 2
    D, G = D_MODEL, POOL_GROUP_DIM
    nrm = jax.random.normal
    f32 = jnp.float32
    return {
        'x': nrm(ks[0], (BATCH, SEQ, D), f32),
        'c': nrm(ks[1], (BATCH, D), f32),
        'norm_mix_gain': 1.0 + 0.02 * nrm(ks[2], (DEPTH, D), f32),
        'norm_ffn_gain': 1.0 + 0.02 * nrm(ks[3], (DEPTH, D), f32),
        'w_ada': 0.5 * D ** -0.5 * nrm(ks[4], (DEPTH, D, N_MOD * D), f32),
        'b_ada': 0.02 * nrm(ks[5], (DEPTH, N_MOD * D), f32),
        'pool_w': G ** -0.5 * nrm(ks[6], (n_pool, N_POOL_GROUPS, G, G), f32),
        'pool_scale': 1.0 + 0.02 * nrm(ks[7], (n_pool, D), f32),
        'hgrn_w_in': D ** -0.5 * nrm(ks[8], (n_hgrn, D, 4 * D), f32),
        'hgrn_w_out': D ** -0.5 * nrm(ks[9], (n_hgrn, D, D), f32),
        'hgrn_norm_gain': 1.0 + 0.02 * nrm(ks[10], (n_hgrn, D), f32),
        'hgrn_lb': 1.0 + 0.1 * nrm(ks[11], (DEPTH, D), f32),
        'w_ffn_in': D ** -0.5 * nrm(ks[12], (DEPTH, D, 2 * D_FF), f32),
        'w_ffn_out': D_FF ** -0.5 * nrm(ks[13], (DEPTH, D_FF, D), f32),
        'final_gain': 1.0 + 0.02 * nrm(ks[14], (D,), f32),
    }


def reference(x, c, norm_mix_gain, norm_ffn_gain, w_ada, b_ada, pool_w, pool_scale,
              hgrn_w_in, hgrn_w_out, hgrn_norm_gain, hgrn_lb, w_ffn_in, w_ffn_out, final_gain):
    lower_bounds = hgrn_lower_bounds(hgrn_lb)
    cond = jax.nn.silu(c)
    for layer in range(DEPTH):
        mod = cond @ w_ada[layer] + b_ada[layer]
        sh_m, sc_m, g_m, sh_f, sc_f, g_f = jnp.split(mod[:, None, :], N_MOD, axis=-1)
        h = rms_norm(x, norm_mix_gain[layer]) * (1.0 + sc_m) + sh_m
        j = layer // N_MIXERS
        if layer % N_MIXERS == 0:
            y = pool_mixer(h, pool_w[j], pool_scale[j])
        else:
            y = hgrn2_mixer(h, hgrn_w_in[j], hgrn_w_out[j], hgrn_norm_gain[j],
                            lower_bounds[layer].astype(jnp.float32))
        x = x + g_m * y
        h = rms_norm(x, norm_ffn_gain[layer]) * (1.0 + sc_f) + sh_f
        x = x + g_f * swiglu(h, w_ffn_in[layer], w_ffn_out[layer])
    return rms_norm(x, final_gain)
```

```python
import functools

import numpy as np
import jax
import jax.numpy as jnp
from jax import lax
from jax.experimental import pallas as pl
from jax.experimental.pallas import tpu as pltpu

EPS = 1e-6
POOL_WINDOWS = (2, 4, 8, 16)
POOL_HALO = 16
HEAD_DIM = 128
N_MOD = 6
REC_CHUNK = 128
REC_LEVELS = 7
V7X_VMEM_LIMIT = 56 * 1024 * 1024

F32 = jnp.float32
BF16 = jnp.bfloat16


def _sigmoid(x):
    return 1.0 / (1.0 + jnp.exp(-x))


def _mod_norm(x, gain, shift, scale):
    ms = jnp.mean(x * x, axis=-1, keepdims=True)
    return x * lax.rsqrt(ms + EPS) * gain * (1.0 + scale) + shift


def _dot(a, b):
    return jnp.dot(a, b, preferred_element_type=F32)


def _dot_nt(a, b):
    return lax.dot_general(a, b, (((1,), (1,)), ((), ())), preferred_element_type=F32)


def _dot_tn(a, b):
    return lax.dot_general(a, b, (((0,), (0,)), ((), ())), preferred_element_type=F32)


def _mod_kernel(c_ref, w_ref, b_ref, o_ref, *, batch):
    cpad = c_ref[...]
    cond = (cpad * _sigmoid(cpad)).astype(BF16)
    res = _dot(cond, w_ref[...].astype(BF16)) + b_ref[...]
    o_ref[...] = res[:batch]


def _modulation(c, w_ada, b_ada):
    depth, d, n = w_ada.shape
    batch = c.shape[0]
    rows = 16
    cpad = jnp.zeros((rows, d), F32).at[:batch].set(c)
    tn = min(n, 1024)
    return pl.pallas_call(
        functools.partial(_mod_kernel, batch=batch),
        grid=(depth, n // tn),
        in_specs=[
            pl.BlockSpec((rows, d), lambda l, j: (0, 0)),
            pl.BlockSpec((None, d, tn), lambda l, j: (l, 0, j)),
            pl.BlockSpec((None, 1, tn), lambda l, j: (l, 0, j)),
        ],
        out_specs=pl.BlockSpec((None, batch, tn), lambda l, j: (l, 0, j)),
        out_shape=jax.ShapeDtypeStruct((depth, batch, n), F32),
        compiler_params=pltpu.CompilerParams(
            dimension_semantics=("parallel", "parallel"),
            vmem_limit_bytes=V7X_VMEM_LIMIT),
        name="adaln_mod",
    )(cpad, w_ada, b_ada.reshape(depth, 1, n))


def _pool_mixer_tile(x, xh_ref, gain, shift, scale, pw_ref, ps_ref, hs_ref, first_tile, pos_base):
    tm, d = x.shape
    g_dim = d // len(POOL_WINDOWS)
    h = _mod_norm(x, gain, shift, scale)
    hh = _mod_norm(xh_ref[...], gain, shift, scale)
    hs_ref[0:POOL_HALO, :] = jnp.where(first_tile, 0.0, hh)
    hs_ref[POOL_HALO:, :] = h
    row = lax.broadcasted_iota(jnp.int32, (POOL_HALO, g_dim), 0)
    pos = (pos_base + row + 1).astype(F32)
    ys = []
    for g, w in enumerate(POOL_WINDOWS):
        lanes = slice(g * g_dim, (g + 1) * g_dim)
        hg = hs_ref[POOL_HALO:, lanes]
        s = hg
        for j in range(1, w):
            s = s + hs_ref[POOL_HALO - j:POOL_HALO - j + tm, lanes]
        top = s[:POOL_HALO] / jnp.minimum(pos, float(w)) - hg[:POOL_HALO]
        rest = s[POOL_HALO:] * (1.0 / w) - hg[POOL_HALO:]
        dg = jnp.concatenate([top, rest], axis=0).astype(BF16)
        ys.append(_dot(dg, pw_ref[g]))
    return jnp.concatenate(ys, axis=1) * ps_ref[...]


def _ffn_kernel(*refs, mixer, final, nf, tiles_per_seq, tm):
    if mixer == "pool":
        (x_ref, xh_ref, mod_ref, gm_ref, gf_ref, pw_ref, ps_ref, wa_ref, wb_ref, wo_ref) = refs[:10]
        rest = refs[10:]
    else:
        (x_ref, og_ref, mod_ref, gf_ref, wout_ref, wa_ref, wb_ref, wo_ref) = refs[:8]
        rest = refs[8:]
    if final:
        fg_ref, rest = rest[0], rest[1:]
    if mixer == "pool":
        o_ref, h2_ref, acc_ref, hs_ref = rest
    else:
        o_ref, h2_ref, acc_ref = rest

    i = pl.program_id(0)
    f = pl.program_id(1)
    tile_in_seq = i % tiles_per_seq

    @pl.when(f == 0)
    def _():
        x = x_ref[...]
        if mixer == "pool":
            y = _pool_mixer_tile(x, xh_ref, gm_ref[...], mod_ref[0:1, :], mod_ref[1:2, :],
                                 pw_ref, ps_ref, hs_ref, tile_in_seq == 0, tile_in_seq * tm)
        else:
            y = _dot(og_ref[...], wout_ref[...])
        xm = x + mod_ref[2:3, :] * y
        o_ref[...] = xm
        h2 = _mod_norm(xm, gf_ref[...], mod_ref[3:4, :], mod_ref[4:5, :])
        h2_ref[...] = h2.astype(BF16)
        acc_ref[...] = jnp.zeros_like(acc_ref)

    h2 = h2_ref[...]
    a = _dot(h2, wa_ref[...])
    b = _dot(h2, wb_ref[...])
    u = (a * _sigmoid(a) * b).astype(BF16)
    acc_ref[...] += _dot(u, wo_ref[...])

    @pl.when(f == nf - 1)
    def _():
        out = o_ref[...] + mod_ref[5:6, :] * acc_ref[...]
        if final:
            ms = jnp.mean(out * out, axis=-1, keepdims=True)
            out = out * lax.rsqrt(ms + EPS) * fg_ref[...]
        o_ref[...] = out


def _ffn_layer(x2, mod_l, norm_ffn_gain_l, w_in, w_out, *, seq, mixer, mixer_args, final_gain=None,
               tm=512, tf=512):
    m, d = x2.shape
    d_ff = w_out.shape[0]
    tm = min(tm, seq)
    tf = min(tf, d_ff)
    assert seq % tm == 0 and d_ff % tf == 0 and tm % POOL_HALO == 0
    nf = d_ff // tf
    tiles_per_seq = seq // tm
    row = lambda v: v.reshape(1, d)
    x_spec = pl.BlockSpec((tm, d), lambda i, f: (i, 0))
    mod_spec = pl.BlockSpec((None, N_MOD, d), lambda i, f: (i // tiles_per_seq, 0, 0))
    vec_spec = pl.BlockSpec((1, d), lambda i, f: (0, 0))
    ffn_specs = [
        pl.BlockSpec((d, tf), lambda i, f: (0, f)),
        pl.BlockSpec((d, tf), lambda i, f: (0, nf + f)),
        pl.BlockSpec((tf, d), lambda i, f: (f, 0)),
    ]
    scratch = [pltpu.VMEM((tm, d), BF16), pltpu.VMEM((tm, d), F32)]
    if mixer == "pool":
        norm_mix_gain_l, pool_w, pool_scale = mixer_args
        n_groups, g_dim, _ = pool_w.shape
        halo_blocks = tm // POOL_HALO
        args = [x2, x2, mod_l, row(norm_mix_gain_l), row(norm_ffn_gain_l), pool_w, row(pool_scale)]
        specs = [x_spec,
                 pl.BlockSpec((POOL_HALO, d), lambda i, f: (jnp.maximum(i * halo_blocks - 1, 0), 0)),
                 mod_spec, vec_spec, vec_spec,
                 pl.BlockSpec((n_groups, g_dim, g_dim), lambda i, f: (0, 0, 0)),
                 vec_spec]
        scratch.append(pltpu.VMEM((tm + POOL_HALO, d), F32))
    else:
        og, w_mix_out = mixer_args
        args = [x2, og, mod_l, row(norm_ffn_gain_l), w_mix_out]
        specs = [x_spec, pl.BlockSpec((tm, d), lambda i, f: (i, 0)), mod_spec, vec_spec,
                 pl.BlockSpec((d, d), lambda i, f: (0, 0), pipeline_mode=pl.Buffered(1))]
    args += [w_in, w_in, w_out]
    specs += ffn_specs
    if final_gain is not None:
        args.append(row(final_gain))
        specs.append(vec_spec)
    return pl.pallas_call(
        functools.partial(_ffn_kernel, mixer=mixer, final=final_gain is not None, nf=nf,
                          tiles_per_seq=tiles_per_seq, tm=tm),
        grid=(m // tm, nf),
        in_specs=specs,
        out_specs=pl.BlockSpec((tm, d), lambda i, f: (i, 0)),
        out_shape=jax.ShapeDtypeStruct((m, d), F32),
        scratch_shapes=scratch,
        compiler_params=pltpu.CompilerParams(
            dimension_semantics=("parallel", "arbitrary"),
            vmem_limit_bytes=V7X_VMEM_LIMIT),
        name="mixer_out_ffn_" + mixer,
    )(*args)


def _proj_kernel(x_ref, mod_ref, g_ref, w_ref, o_ref, h_ref):
    @pl.when(pl.program_id(1) == 0)
    def _():
        h = _mod_norm(x_ref[...], g_ref[...], mod_ref[0:1, :], mod_ref[1:2, :])
        h_ref[...] = h.astype(BF16)

    o_ref[...] = _dot(h_ref[...], w_ref[...])


def _hgrn_proj(x2, mod_l, norm_gain_l, w_in, *, seq, tm=512, tn=1024):
    m, d = x2.shape
    n = w_in.shape[1]
    tm = min(tm, seq)
    tn = min(tn, n)
    tiles_per_seq = seq // tm
    return pl.pallas_call(
        _proj_kernel,
        grid=(m // tm, n // tn),
        in_specs=[
            pl.BlockSpec((tm, d), lambda i, j: (i, 0)),
            pl.BlockSpec((None, N_MOD, d), lambda i, j: (i // tiles_per_seq, 0, 0)),
            pl.BlockSpec((1, d), lambda i, j: (0, 0)),
            pl.BlockSpec((d, tn), lambda i, j: (0, j)),
        ],
        out_specs=pl.BlockSpec((tm, tn), lambda i, j: (i, j)),
        out_shape=jax.ShapeDtypeStruct((m, n), F32),
        scratch_shapes=[pltpu.VMEM((tm, d), BF16)],
        compiler_params=pltpu.CompilerParams(
            dimension_semantics=("parallel", "arbitrary"),
            vmem_limit_bytes=V7X_VMEM_LIMIT),
        name="hgrn_proj",
    )(x2, mod_l, norm_gain_l.reshape(1, d), w_in)


def _rec_constants():
    c = REC_CHUNK
    t = np.arange(c)[:, None]
    u = np.arange(c)[None, :]
    mats = []
    level = np.full((c, c), -1, np.int32)
    for l in range(REC_LEVELS):
        n = c >> l
        half = n // 2
        mid = (t // n) * n + half
        second = t >= mid
        mats.append(np.where(second, (u >= mid) & (u <= t), (u > t) & (u < mid)))
        same = (t // n) == (u // n)
        level[same & (t % n >= half) & (u % n < half)] = l
    mats.append(u <= t)
    mats.append(u > t)
    level[np.arange(c), np.arange(c)] = REC_LEVELS
    return np.concatenate(mats, axis=0).astype(np.float32), level


def _rec_kernel(q_ref, f_ref, v_ref, g_ref, lbp_ref, gain_ref, sums_ref, lvl_ref, o_ref, st_ref,
                *, layer, heads):
    c = REC_CHUNK

    @pl.when(pl.program_id(2) == 0)
    def _():
        st_ref[...] = jnp.zeros_like(st_ref)

    lbp = lbp_ref[...]
    e = jnp.exp(lbp - jnp.max(lbp, axis=0, keepdims=True))
    p = e / jnp.sum(e, axis=0, keepdims=True)
    lb_all = jnp.sum(p[1:layer + 1], axis=0, keepdims=True) if layer > 0 else jnp.zeros_like(p[0:1])

    lvl = lvl_ref[...]
    rowi = lax.broadcasted_iota(jnp.int32, (c, HEAD_DIM), 0)
    sums = sums_ref[...]
    for j in range(heads):
        lanes = slice(j * HEAD_DIM, (j + 1) * HEAD_DIM)
        q = q_ref[:, lanes]
        v = v_ref[:, lanes].astype(BF16)
        gate = g_ref[:, lanes]
        lb = lb_all[:, lanes]
        fgt = lb + (1.0 - lb) * _sigmoid(f_ref[:, lanes])
        k = 1.0 - fgt
        logf = jnp.log(fgt)
        hi = logf.astype(BF16)
        lo = (logf - hi.astype(F32)).astype(BF16)
        part = _dot(sums, jnp.concatenate([hi, lo], axis=1))
        ex = part[:, :HEAD_DIM] + part[:, HEAD_DIM:]

        scores = jnp.zeros((c, c), F32)
        for l in range(REC_LEVELS):
            half = c >> (l + 1)
            fac = jnp.exp(ex[l * c:(l + 1) * c])
            xl = (jnp.where((rowi & half) != 0, q, k) * fac).astype(BF16)
            scores = jnp.where(lvl == l, _dot_nt(xl, xl), scores)
        scores = jnp.where(lvl == REC_LEVELS, _dot_nt(q.astype(BF16), k.astype(BF16)), scores)

        cum = ex[REC_LEVELS * c:(REC_LEVELS + 1) * c]
        rev = ex[(REC_LEVELS + 1) * c:(REC_LEVELS + 2) * c]
        st = st_ref[j]
        qd = (q * jnp.exp(cum)).astype(BF16)
        o = _dot(scores.astype(BF16), v) + _dot_nt(qd, st.astype(BF16))
        kd = (k * jnp.exp(rev)).astype(BF16)
        st_ref[j] = st * jnp.exp(cum[c - 1:c, :]) + _dot_tn(v, kd)

        ms = jnp.mean(o * o, axis=-1, keepdims=True)
        on = o * lax.rsqrt(ms + EPS) * gain_ref[:, lanes] * (gate * _sigmoid(gate))
        o_ref[:, lanes] = on.astype(BF16)


def _hgrn_recurrence(proj, hgrn_lb, out_norm_gain_l, *, batch, seq, layer, heads_per_step=4):
    m, n4 = proj.shape
    d = n4 // 4
    depth = hgrn_lb.shape[0]
    n_heads = d // HEAD_DIM
    hb = min(heads_per_step, n_heads)
    wl = hb * HEAD_DIM
    groups = n_heads // hb
    c = REC_CHUNK
    assert seq % c == 0 and n_heads % hb == 0
    chunks = seq // c
    sums_np, lvl_np = _rec_constants()
    sums = jnp.asarray(sums_np, BF16)
    lvl = jnp.asarray(lvl_np)

    def col_spec(part):
        return pl.BlockSpec((c, wl), lambda b, g, t: (b * chunks + t, part * groups + g))

    return pl.pallas_call(
        functools.partial(_rec_kernel, layer=layer, heads=hb),
        grid=(batch, groups, chunks),
        in_specs=[
            col_spec(0), col_spec(1), col_spec(2), col_spec(3),
            pl.BlockSpec((depth, wl), lambda b, g, t: (0, g)),
            pl.BlockSpec((1, wl), lambda b, g, t: (0, g)),
            pl.BlockSpec(sums.shape, lambda b, g, t: (0, 0)),
            pl.BlockSpec(lvl.shape, lambda b, g, t: (0, 0)),
        ],
        out_specs=pl.BlockSpec((c, wl), lambda b, g, t: (b * chunks + t, g)),
        out_shape=jax.ShapeDtypeStruct((m, d), BF16),
        scratch_shapes=[pltpu.VMEM((hb, HEAD_DIM, HEAD_DIM), F32)],
        compiler_params=pltpu.CompilerParams(
            dimension_semantics=("parallel", "parallel", "arbitrary"),
            vmem_limit_bytes=V7X_VMEM_LIMIT),
        name="hgrn_recurrence",
    )(proj, proj, proj, proj, hgrn_lb, out_norm_gain_l.reshape(1, d), sums, lvl)


def kernel(x, c, norm_mix_gain, norm_ffn_gain, w_ada, b_ada, pool_w, pool_scale, hgrn_w_in, hgrn_w_out,
           hgrn_norm_gain, hgrn_lb, w_ffn_in, w_ffn_out, final_gain):
    batch, seq, d = x.shape
    depth = w_ada.shape[0]
    mods = _modulation(c, w_ada, b_ada).reshape(depth, batch, N_MOD, d)
    x2 = x.reshape(batch * seq, d)
    for layer in range(depth):
        j = layer // 2
        fin = final_gain if layer == depth - 1 else None
        w_in = w_ffn_in[layer].astype(BF16)
        w_out = w_ffn_out[layer].astype(BF16)
        if layer % 2 == 0:
            mixer_args = (norm_mix_gain[layer], pool_w[j].astype(BF16), pool_scale[j])
            x2 = _ffn_layer(x2, mods[layer], norm_ffn_gain[layer], w_in, w_out, seq=seq,
                            mixer="pool", mixer_args=mixer_args, final_gain=fin)
        else:
            proj = _hgrn_proj(x2, mods[layer], norm_mix_gain[layer], hgrn_w_in[j].astype(BF16), seq=seq)
            og = _hgrn_recurrence(proj, hgrn_lb, hgrn_norm_gain[j], batch=batch, seq=seq, layer=layer)
            mixer_args = (og, hgrn_w_out[j].astype(BF16))
            x2 = _ffn_layer(x2, mods[layer], norm_ffn_gain[layer], w_in, w_out, seq=seq,
                            mixer="hgrn", mixer_args=mixer_args, final_gain=fin)
    return x2.reshape(batch, seq, d)
```

```python
import functools

import numpy as np
import jax
import jax.numpy as jnp
from jax import lax
from jax.experimental import pallas as pl
from jax.experimental.pallas import tpu as pltpu

EPS = 1e-6
POOL_WINDOWS = (2, 4, 8, 16)
POOL_HALO = 16
HEAD_DIM = 128
N_MOD = 6
REC_CHUNK = 128
REC_LEVELS = 7
REC_FINE_LEVELS = (4, 5)
V7X_VMEM_LIMIT = 56 * 1024 * 1024

F32 = jnp.float32
BF16 = jnp.bfloat16


def _sigmoid(x):
    return 1.0 / (1.0 + jnp.exp(-x))


def _mod_norm(x, gain, shift, scale):
    ms = jnp.mean(x * x, axis=-1, keepdims=True)
    return x * lax.rsqrt(ms + EPS) * (gain * (1.0 + scale)) + shift


def _dot(a, b):
    return jnp.dot(a, b, preferred_element_type=F32)


def _dot_nt(a, b):
    return lax.dot_general(a, b, (((1,), (1,)), ((), ())), preferred_element_type=F32)


def _dot_tn(a, b):
    return lax.dot_general(a, b, (((0,), (0,)), ((), ())), preferred_element_type=F32)


def _mod_kernel(c_ref, w_ref, b_ref, o_ref, *, batch):
    cpad = c_ref[...]
    cond = (cpad * _sigmoid(cpad)).astype(BF16)
    res = _dot(cond, w_ref[...].astype(BF16)) + b_ref[...]
    o_ref[...] = res[:batch]


def _modulation(c, w_ada, b_ada):
    depth, d, n = w_ada.shape
    batch = c.shape[0]
    rows = 16
    cpad = jnp.zeros((rows, d), F32).at[:batch].set(c)
    tn = min(n, 1024)
    return pl.pallas_call(
        functools.partial(_mod_kernel, batch=batch),
        grid=(depth, n // tn),
        in_specs=[
            pl.BlockSpec((rows, d), lambda l, j: (0, 0)),
            pl.BlockSpec((None, d, tn), lambda l, j: (l, 0, j)),
            pl.BlockSpec((None, 1, tn), lambda l, j: (l, 0, j)),
        ],
        out_specs=pl.BlockSpec((None, batch, tn), lambda l, j: (l, 0, j)),
        out_shape=jax.ShapeDtypeStruct((depth, batch, n), F32),
        compiler_params=pltpu.CompilerParams(
            dimension_semantics=("parallel", "parallel"),
            vmem_limit_bytes=V7X_VMEM_LIMIT),
        name="adaln_mod",
    )(cpad, w_ada, b_ada.reshape(depth, 1, n))


def _pool_mixer_tile(x, xh_ref, gain, shift, scale, pw_ref, ps_ref, hs_ref, first_tile, pos_base):
    tm, d = x.shape
    g_dim = d // len(POOL_WINDOWS)
    h = _mod_norm(x, gain, shift, scale)
    hh = _mod_norm(xh_ref[...], gain, shift, scale)
    hs_ref[0:POOL_HALO, :] = jnp.where(first_tile, 0.0, hh)
    hs_ref[POOL_HALO:, :] = h
    row = lax.broadcasted_iota(jnp.int32, (POOL_HALO, g_dim), 0)
    pos = (pos_base + row + 1).astype(F32)
    ys = []
    for g, w in enumerate(POOL_WINDOWS):
        lanes = slice(g * g_dim, (g + 1) * g_dim)
        s = hs_ref[:, lanes]
        span = 1
        while span < w:
            s = s + pltpu.roll(s, span, axis=0)
            span *= 2
        s = s[POOL_HALO:]
        hg = hs_ref[POOL_HALO:, lanes]
        top = s[:POOL_HALO] / jnp.minimum(pos, float(w)) - hg[:POOL_HALO]
        rest = s[POOL_HALO:] * (1.0 / w) - hg[POOL_HALO:]
        dg = jnp.concatenate([top, rest], axis=0).astype(BF16)
        ys.append(_dot(dg, pw_ref[g]))
    return jnp.concatenate(ys, axis=1) * ps_ref[...]


def _ffn_kernel(*refs, mixer, final, nf, tiles_per_seq, tm):
    if mixer == "pool":
        (x_ref, xh_ref, mod_ref, gm_ref, gf_ref, pw_ref, ps_ref, wa_ref, wb_ref, wo_ref) = refs[:10]
        rest = refs[10:]
    else:
        (x_ref, og_ref, mod_ref, gf_ref, wout_ref, wa_ref, wb_ref, wo_ref) = refs[:8]
        rest = refs[8:]
    if final:
        fg_ref, rest = rest[0], rest[1:]
    if mixer == "pool":
        o_ref, h2_ref, acc_ref, hs_ref = rest
    else:
        o_ref, h2_ref, acc_ref = rest

    i = pl.program_id(0)
    f = pl.program_id(1)
    tile_in_seq = i % tiles_per_seq

    @pl.when(f == 0)
    def _():
        x = x_ref[...]
        if mixer == "pool":
            y = _pool_mixer_tile(x, xh_ref, gm_ref[...], mod_ref[0:1, :], mod_ref[1:2, :],
                                 pw_ref, ps_ref, hs_ref, tile_in_seq == 0, tile_in_seq * tm)
        else:
            y = _dot(og_ref[...], wout_ref[...])
        xm = x + mod_ref[2:3, :] * y
        o_ref[...] = xm
        h2 = _mod_norm(xm, gf_ref[...], mod_ref[3:4, :], mod_ref[4:5, :])
        h2_ref[...] = h2.astype(BF16)
        acc_ref[...] = jnp.zeros_like(acc_ref)

    h2 = h2_ref[...]
    a = _dot(h2, wa_ref[...])
    b = _dot(h2, wb_ref[...])
    u = (a * _sigmoid(a) * b).astype(BF16)
    acc_ref[...] += _dot(u, wo_ref[...])

    @pl.when(f == nf - 1)
    def _():
        out = o_ref[...] + mod_ref[5:6, :] * acc_ref[...]
        if final:
            ms = jnp.mean(out * out, axis=-1, keepdims=True)
            out = out * lax.rsqrt(ms + EPS) * fg_ref[...]
        o_ref[...] = out


def _ffn_layer(x2, mod_l, norm_ffn_gain_l, w_in, w_out, *, layer, seq, mixer, mixer_args, final_gain=None,
               tm=512, tf=512):
    m, d = x2.shape
    d_ff = w_out.shape[1]
    tm = min(tm, seq)
    tf = min(tf, d_ff)
    assert seq % tm == 0 and d_ff % tf == 0 and tm % POOL_HALO == 0
    nf = d_ff // tf
    tiles_per_seq = seq // tm
    row = lambda v: v.reshape(1, d)
    x_spec = pl.BlockSpec((tm, d), lambda i, f: (i, 0))
    mod_spec = pl.BlockSpec((None, N_MOD, d), lambda i, f: (i // tiles_per_seq, 0, 0))
    vec_spec = pl.BlockSpec((1, d), lambda i, f: (0, 0))
    ffn_specs = [
        pl.BlockSpec((None, d, tf), lambda i, f: (layer, 0, f)),
        pl.BlockSpec((None, d, tf), lambda i, f: (layer, 0, nf + f)),
        pl.BlockSpec((None, tf, d), lambda i, f: (layer, f, 0)),
    ]
    scratch = [pltpu.VMEM((tm, d), BF16), pltpu.VMEM((tm, d), F32)]
    if mixer == "pool":
        norm_mix_gain_l, pool_w, pool_scale, j = mixer_args
        _, n_groups, g_dim, _ = pool_w.shape
        halo_blocks = tm // POOL_HALO
        args = [x2, x2, mod_l, row(norm_mix_gain_l), row(norm_ffn_gain_l), pool_w, row(pool_scale)]
        specs = [x_spec,
                 pl.BlockSpec((POOL_HALO, d), lambda i, f: (jnp.maximum(i * halo_blocks - 1, 0), 0)),
                 mod_spec, vec_spec, vec_spec,
                 pl.BlockSpec((None, n_groups, g_dim, g_dim), lambda i, f: (j, 0, 0, 0)),
                 vec_spec]
        scratch.append(pltpu.VMEM((tm + POOL_HALO, d), F32))
    else:
        og, w_mix_out, j = mixer_args
        args = [x2, og, mod_l, row(norm_ffn_gain_l), w_mix_out]
        specs = [x_spec, pl.BlockSpec((tm, d), lambda i, f: (i, 0)), mod_spec, vec_spec,
                 pl.BlockSpec((None, d, d), lambda i, f: (j, 0, 0), pipeline_mode=pl.Buffered(1))]
    args += [w_in, w_in, w_out]
    specs += ffn_specs
    if final_gain is not None:
        args.append(row(final_gain))
        specs.append(vec_spec)
    return pl.pallas_call(
        functools.partial(_ffn_kernel, mixer=mixer, final=final_gain is not None, nf=nf,
                          tiles_per_seq=tiles_per_seq, tm=tm),
        grid=(m // tm, nf),
        in_specs=specs,
        out_specs=pl.BlockSpec((tm, d), lambda i, f: (i, 0)),
        out_shape=jax.ShapeDtypeStruct((m, d), F32),
        scratch_shapes=scratch,
        compiler_params=pltpu.CompilerParams(
            dimension_semantics=("parallel", "arbitrary"),
            vmem_limit_bytes=V7X_VMEM_LIMIT),
        name="mixer_out_ffn_" + mixer,
    )(*args)


def _proj_kernel(x_ref, mod_ref, g_ref, w_ref, o_ref, h_ref):
    @pl.when(pl.program_id(1) == 0)
    def _():
        h = _mod_norm(x_ref[...], g_ref[...], mod_ref[0:1, :], mod_ref[1:2, :])
        h_ref[...] = h.astype(BF16)

    o_ref[...] = _dot(h_ref[...], w_ref[...])


def _hgrn_proj(x2, mod_l, norm_gain_l, w_in, *, layer_idx, seq, tm=1024, tn=1024):
    m, d = x2.shape
    n = w_in.shape[2]
    tm = min(tm, seq)
    tn = min(tn, n)
    tiles_per_seq = seq // tm
    return pl.pallas_call(
        _proj_kernel,
        grid=(m // tm, n // tn),
        in_specs=[
            pl.BlockSpec((tm, d), lambda i, j: (i, 0)),
            pl.BlockSpec((None, N_MOD, d), lambda i, j: (i // tiles_per_seq, 0, 0)),
            pl.BlockSpec((1, d), lambda i, j: (0, 0)),
            pl.BlockSpec((None, d, tn), lambda i, j: (layer_idx, 0, j)),
        ],
        out_specs=pl.BlockSpec((tm, tn), lambda i, j: (i, j)),
        out_shape=jax.ShapeDtypeStruct((m, n), F32),
        scratch_shapes=[pltpu.VMEM((tm, d), BF16)],
        compiler_params=pltpu.CompilerParams(
            dimension_semantics=("parallel", "arbitrary"),
            vmem_limit_bytes=V7X_VMEM_LIMIT),
        name="hgrn_proj",
    )(x2, mod_l, norm_gain_l.reshape(1, d), w_in)


def _level_split(l):
    n = REC_CHUNK >> l
    return n, n // 2


def _rec_constants():
    c = REC_CHUNK
    t = np.arange(c)[:, None]
    u = np.arange(c)[None, :]
    mats = [u <= t]
    level = np.full((c, c), -1, np.int32)
    for l in range(REC_LEVELS):
        n, half = _level_split(l)
        mid = (t // n) * n + half
        if l in REC_FINE_LEVELS:
            mats.append(np.where(t >= mid, (u >= mid) & (u <= t), (u > t) & (u < mid)))
        same = (t // n) == (u // n)
        level[same & (t % n >= half) & (u % n < half)] = l
    level[np.arange(c), np.arange(c)] = REC_LEVELS
    return np.concatenate(mats, axis=0).astype(np.float32), level


def _coarse_level_operand(q, k, cum, l):
    n, half = _level_split(l)
    pieces = []
    for lo in range(0, REC_CHUNK, n):
        mid, hi = lo + half, lo + n
        ref = cum[mid - 1:mid]
        pieces.append(k[lo:mid] * jnp.exp(ref - cum[lo:mid]))
        pieces.append(q[mid:hi] * jnp.exp(cum[mid:hi] - ref))
    return jnp.concatenate(pieces, axis=0)


def _rec_kernel(q_ref, f_ref, v_ref, g_ref, lbp_ref, gain_ref, sums_ref, lvl_ref, o_ref, st_ref,
                *, layer, heads):
    c = REC_CHUNK

    @pl.when(pl.program_id(2) == 0)
    def _():
        st_ref[...] = jnp.zeros_like(st_ref)

    lbp = lbp_ref[...]
    e = jnp.exp(lbp - jnp.max(lbp, axis=0, keepdims=True))
    p = e / jnp.sum(e, axis=0, keepdims=True)
    lb_all = jnp.sum(p[1:layer + 1], axis=0, keepdims=True) if layer > 0 else jnp.zeros_like(p[0:1])

    lvl = lvl_ref[...]
    in_level = [lvl == l for l in range(REC_LEVELS + 1)]
    rowi = lax.broadcasted_iota(jnp.int32, (c, HEAD_DIM), 0)
    second_half = {l: (rowi & _level_split(l)[1]) != 0 for l in REC_FINE_LEVELS + (REC_LEVELS - 1,)}
    sums = sums_ref[...]
    for j in range(heads):
        lanes = slice(j * HEAD_DIM, (j + 1) * HEAD_DIM)
        q = q_ref[:, lanes]
        v = v_ref[:, lanes].astype(BF16)
        gate = g_ref[:, lanes]
        lb = lb_all[:, lanes]
        fgt = lb + (1.0 - lb) * _sigmoid(f_ref[:, lanes])
        k = 1.0 - fgt
        logf = jnp.log(fgt)
        hi = logf.astype(BF16)
        lo = (logf - hi.astype(F32)).astype(BF16)
        part = _dot(sums, jnp.concatenate([hi, lo], axis=1))
        ex = part[:, :HEAD_DIM] + part[:, HEAD_DIM:]
        cum = ex[:c]

        scores = jnp.zeros((c, c), F32)
        for l in range(REC_LEVELS):
            if l in REC_FINE_LEVELS:
                i = 1 + REC_FINE_LEVELS.index(l)
                xl = jnp.where(second_half[l], q, k) * jnp.exp(ex[i * c:(i + 1) * c])
            elif l == REC_LEVELS - 1:
                xl = jnp.where(second_half[l], q * fgt, k)
            else:
                xl = _coarse_level_operand(q, k, cum, l)
            xl = xl.astype(BF16)
            scores = jnp.where(in_level[l], _dot_nt(xl, xl), scores)
        scores = jnp.where(in_level[REC_LEVELS], _dot_nt(q.astype(BF16), k.astype(BF16)), scores)

        st = st_ref[j]
        end = cum[c - 1:c]
        qd = (q * jnp.exp(cum)).astype(BF16)
        o = _dot(scores.astype(BF16), v) + _dot_nt(qd, st.astype(BF16))
        kd = (k * jnp.exp(end - cum)).astype(BF16)
        st_ref[j] = st * jnp.exp(end) + _dot_tn(v, kd)

        ms = jnp.mean(o * o, axis=-1, keepdims=True)
        on = o * lax.rsqrt(ms + EPS) * gain_ref[:, lanes] * (gate * _sigmoid(gate))
        o_ref[:, lanes] = on.astype(BF16)


def _hgrn_recurrence(proj, hgrn_lb, out_norm_gain_l, *, batch, seq, layer, heads_per_step=8):
    m, n4 = proj.shape
    d = n4 // 4
    depth = hgrn_lb.shape[0]
    n_heads = d // HEAD_DIM
    hb = min(heads_per_step, n_heads)
    wl = hb * HEAD_DIM
    groups = n_heads // hb
    c = REC_CHUNK
    assert seq % c == 0 and n_heads % hb == 0
    chunks = seq // c
    sums_np, lvl_np = _rec_constants()
    sums = jnp.asarray(sums_np, BF16)
    lvl = jnp.asarray(lvl_np)

    def col_spec(part):
        return pl.BlockSpec((c, wl), lambda b, g, t: (b * chunks + t, part * groups + g))

    return pl.pallas_call(
        functools.partial(_rec_kernel, layer=layer, heads=hb),
        grid=(batch, groups, chunks),
        in_specs=[
            col_spec(0), col_spec(1), col_spec(2), col_spec(3),
            pl.BlockSpec((depth, wl), lambda b, g, t: (0, g)),
            pl.BlockSpec((1, wl), lambda b, g, t: (0, g)),
            pl.BlockSpec(sums.shape, lambda b, g, t: (0, 0)),
            pl.BlockSpec(lvl.shape, lambda b, g, t: (0, 0)),
        ],
        out_specs=pl.BlockSpec((c, wl), lambda b, g, t: (b * chunks + t, g)),
        out_shape=jax.ShapeDtypeStruct((m, d), BF16),
        scratch_shapes=[pltpu.VMEM((hb, HEAD_DIM, HEAD_DIM), F32)],
        compiler_params=pltpu.CompilerParams(
            dimension_semantics=("parallel", "parallel", "arbitrary"),
            vmem_limit_bytes=V7X_VMEM_LIMIT),
        name="hgrn_recurrence",
    )(proj, proj, proj, proj, hgrn_lb, out_norm_gain_l.reshape(1, d), sums, lvl)


def kernel(x, c, norm_mix_gain, norm_ffn_gain, w_ada, b_ada, pool_w, pool_scale, hgrn_w_in, hgrn_w_out,
           hgrn_norm_gain, hgrn_lb, w_ffn_in, w_ffn_out, final_gain):
    batch, seq, d = x.shape
    depth = w_ada.shape[0]
    mods = _modulation(c, w_ada, b_ada).reshape(depth, batch, N_MOD, d)
    x2 = x.reshape(batch * seq, d)
    w_in16, w_out16 = w_ffn_in.astype(BF16), w_ffn_out.astype(BF16)
    pool_w16, hgrn_in16, hgrn_out16 = pool_w.astype(BF16), hgrn_w_in.astype(BF16), hgrn_w_out.astype(BF16)
    for layer in range(depth):
        j = layer // 2
        fin = final_gain if layer == depth - 1 else None
        if layer % 2 == 0:
            mixer_args = (norm_mix_gain[layer], pool_w16, pool_scale[j], j)
            x2 = _ffn_layer(x2, mods[layer], norm_ffn_gain[layer], w_in16, w_out16, layer=layer, seq=seq,
                            mixer="pool", mixer_args=mixer_args, final_gain=fin)
        else:
            proj = _hgrn_proj(x2, mods[layer], norm_mix_gain[layer], hgrn_in16, layer_idx=j, seq=seq)
            og = _hgrn_recurrence(proj, hgrn_lb, hgrn_norm_gain[j], batch=batch, seq=seq, layer=layer)
            mixer_args = (og, hgrn_out16, j)
            x2 = _ffn_layer(x2, mods[layer], norm_ffn_gain[layer], w_in16, w_out16, layer=layer, seq=seq,
                            mixer="hgrn", mixer_args=mixer_args, final_gain=fin)
    return x2.reshape(batch, seq, d)
```

```python
import functools

import numpy as np
import jax
import jax.numpy as jnp
from jax import lax
from jax.experimental import pallas as pl
from jax.experimental.pallas import tpu as pltpu

EPS = 1e-6
POOL_WINDOWS = (2, 4, 8, 16)
POOL_HALO = 16
HEAD_DIM = 128
N_MOD = 6
N_PROJ = 4
REC_HEADS = 8
REC_CHUNK = 128
REC_LEVELS = 7
REC_FINE_LEVELS = (4, 5)
V7X_VMEM_LIMIT = 56 * 1024 * 1024

F32 = jnp.float32
BF16 = jnp.bfloat16


def _sigmoid(x):
    return 1.0 / (1.0 + jnp.exp(-x))


def _mod_norm(x, gain, shift, scale):
    ms = jnp.mean(x * x, axis=-1, keepdims=True)
    return x * lax.rsqrt(ms + EPS) * (gain * (1.0 + scale)) + shift


def _dot(a, b):
    return jnp.dot(a, b, preferred_element_type=F32)


def _dot_nt(a, b):
    return lax.dot_general(a, b, (((1,), (1,)), ((), ())), preferred_element_type=F32)


def _dot_tn(a, b):
    return lax.dot_general(a, b, (((0,), (0,)), ((), ())), preferred_element_type=F32)


def _mod_kernel(c_ref, w_ref, b_ref, o_ref, *, batch):
    cpad = c_ref[...]
    cond = (cpad * _sigmoid(cpad)).astype(BF16)
    res = _dot(cond, w_ref[...].astype(BF16)) + b_ref[...]
    o_ref[...] = res[:batch]


def _modulation(c, w_ada, b_ada):
    depth, d, n = w_ada.shape
    batch = c.shape[0]
    rows = 16
    cpad = jnp.zeros((rows, d), F32).at[:batch].set(c)
    tn = min(n, 1024)
    return pl.pallas_call(
        functools.partial(_mod_kernel, batch=batch),
        grid=(depth, n // tn),
        in_specs=[
            pl.BlockSpec((rows, d), lambda l, j: (0, 0)),
            pl.BlockSpec((None, d, tn), lambda l, j: (l, 0, j)),
            pl.BlockSpec((None, 1, tn), lambda l, j: (l, 0, j)),
        ],
        out_specs=pl.BlockSpec((None, batch, tn), lambda l, j: (l, 0, j)),
        out_shape=jax.ShapeDtypeStruct((depth, batch, n), F32),
        compiler_params=pltpu.CompilerParams(
            dimension_semantics=("parallel", "parallel"),
            vmem_limit_bytes=V7X_VMEM_LIMIT),
        name="adaln_mod",
    )(cpad, w_ada, b_ada.reshape(depth, 1, n))


def _pool_mixer_tile(x, xh_ref, gain, shift, scale, pw_ref, ps_ref, hs_ref, first_tile, pos_base):
    tm, d = x.shape
    g_dim = d // len(POOL_WINDOWS)
    h = _mod_norm(x, gain, shift, scale)
    hh = _mod_norm(xh_ref[...], gain, shift, scale)
    hs_ref[0:POOL_HALO, :] = jnp.where(first_tile, 0.0, hh)
    hs_ref[POOL_HALO:, :] = h
    row = lax.broadcasted_iota(jnp.int32, (POOL_HALO, g_dim), 0)
    pos = (pos_base + row + 1).astype(F32)
    ys = []
    for g, w in enumerate(POOL_WINDOWS):
        lanes = slice(g * g_dim, (g + 1) * g_dim)
        s = hs_ref[:, lanes]
        span = 1
        while span < w:
            s = s + pltpu.roll(s, span, axis=0)
            span *= 2
        s = s[POOL_HALO:]
        hg = hs_ref[POOL_HALO:, lanes]
        top = s[:POOL_HALO] / jnp.minimum(pos, float(w)) - hg[:POOL_HALO]
        rest = s[POOL_HALO:] * (1.0 / w) - hg[POOL_HALO:]
        dg = jnp.concatenate([top, rest], axis=0).astype(BF16)
        ys.append(_dot(dg, pw_ref[g]))
    return jnp.concatenate(ys, axis=1) * ps_ref[...]


def _ffn_kernel(*refs, mixer, final, nf, tiles_per_seq, tm):
    if mixer == "pool":
        (x_ref, xh_ref, mod_ref, gm_ref, gf_ref, pw_ref, ps_ref, wa_ref, wb_ref, wo_ref) = refs[:10]
        rest = refs[10:]
    else:
        (x_ref, og_ref, mod_ref, gf_ref, wout_ref, wa_ref, wb_ref, wo_ref) = refs[:8]
        rest = refs[8:]
    if final:
        fg_ref, rest = rest[0], rest[1:]
    if mixer == "pool":
        o_ref, h2_ref, acc_ref, hs_ref = rest
    else:
        o_ref, h2_ref, acc_ref = rest

    i = pl.program_id(0)
    f = pl.program_id(1)
    tile_in_seq = i % tiles_per_seq

    @pl.when(f == 0)
    def _():
        x = x_ref[...]
        if mixer == "pool":
            y = _pool_mixer_tile(x, xh_ref, gm_ref[...], mod_ref[0:1, :], mod_ref[1:2, :],
                                 pw_ref, ps_ref, hs_ref, tile_in_seq == 0, tile_in_seq * tm)
        else:
            y = _dot(og_ref[...], wout_ref[...])
        xm = x + mod_ref[2:3, :] * y
        o_ref[...] = xm
        h2 = _mod_norm(xm, gf_ref[...], mod_ref[3:4, :], mod_ref[4:5, :])
        h2_ref[...] = h2.astype(BF16)
        acc_ref[...] = jnp.zeros_like(acc_ref)

    h2 = h2_ref[...]
    a = _dot(h2, wa_ref[...])
    b = _dot(h2, wb_ref[...])
    u = (a * _sigmoid(a) * b).astype(BF16)
    acc_ref[...] += _dot(u, wo_ref[...])

    @pl.when(f == nf - 1)
    def _():
        out = o_ref[...] + mod_ref[5:6, :] * acc_ref[...]
        if final:
            ms = jnp.mean(out * out, axis=-1, keepdims=True)
            out = out * lax.rsqrt(ms + EPS) * fg_ref[...]
        o_ref[...] = out


def _ffn_layer(x2, mod_l, norm_ffn_gain_l, w_in, w_out, *, layer, seq, mixer, mixer_args, final_gain=None,
               tm=512, tf=512):
    m, d = x2.shape
    d_ff = w_out.shape[1]
    tm = min(tm, seq)
    tf = min(tf, d_ff)
    assert seq % tm == 0 and d_ff % tf == 0 and tm % POOL_HALO == 0
    nf = d_ff // tf
    tiles_per_seq = seq // tm
    row = lambda v: v.reshape(1, d)
    x_spec = pl.BlockSpec((tm, d), lambda i, f: (i, 0))
    mod_spec = pl.BlockSpec((None, N_MOD, d), lambda i, f: (i // tiles_per_seq, 0, 0))
    vec_spec = pl.BlockSpec((1, d), lambda i, f: (0, 0))
    ffn_specs = [
        pl.BlockSpec((None, d, tf), lambda i, f: (layer, 0, f)),
        pl.BlockSpec((None, d, tf), lambda i, f: (layer, 0, nf + f)),
        pl.BlockSpec((None, tf, d), lambda i, f: (layer, f, 0)),
    ]
    scratch = [pltpu.VMEM((tm, d), BF16), pltpu.VMEM((tm, d), F32)]
    if mixer == "pool":
        norm_mix_gain_l, pool_w, pool_scale, j = mixer_args
        _, n_groups, g_dim, _ = pool_w.shape
        halo_blocks = tm // POOL_HALO
        args = [x2, x2, mod_l, row(norm_mix_gain_l), row(norm_ffn_gain_l), pool_w, row(pool_scale)]
        specs = [x_spec,
                 pl.BlockSpec((POOL_HALO, d), lambda i, f: (jnp.maximum(i * halo_blocks - 1, 0), 0)),
                 mod_spec, vec_spec, vec_spec,
                 pl.BlockSpec((None, n_groups, g_dim, g_dim), lambda i, f: (j, 0, 0, 0)),
                 vec_spec]
        scratch.append(pltpu.VMEM((tm + POOL_HALO, d), F32))
    else:
        og, w_mix_out, j = mixer_args
        args = [x2, og, mod_l, row(norm_ffn_gain_l), w_mix_out]
        specs = [x_spec, pl.BlockSpec((tm, d), lambda i, f: (i, 0)), mod_spec, vec_spec,
                 pl.BlockSpec((None, d, d), lambda i, f: (j, 0, 0), pipeline_mode=pl.Buffered(1))]
    args += [w_in, w_in, w_out]
    specs += ffn_specs
    if final_gain is not None:
        args.append(row(final_gain))
        specs.append(vec_spec)
    return pl.pallas_call(
        functools.partial(_ffn_kernel, mixer=mixer, final=final_gain is not None, nf=nf,
                          tiles_per_seq=tiles_per_seq, tm=tm),
        grid=(m // tm, nf),
        in_specs=specs,
        out_specs=pl.BlockSpec((tm, d), lambda i, f: (i, 0)),
        out_shape=jax.ShapeDtypeStruct((m, d), F32),
        scratch_shapes=scratch,
        compiler_params=pltpu.CompilerParams(
            dimension_semantics=("parallel", "arbitrary"),
            vmem_limit_bytes=V7X_VMEM_LIMIT),
        name="mixer_out_ffn_" + mixer,
    )(*args)


def _norm_kernel(x_ref, mod_ref, g_ref, o_ref):
    h = _mod_norm(x_ref[...], g_ref[...], mod_ref[0:1, :], mod_ref[1:2, :])
    o_ref[...] = h.astype(BF16)


def _hgrn_norm(x2, mod_l, norm_gain_l, *, seq, tm=512):
    m, d = x2.shape
    tm = min(tm, seq)
    tiles_per_seq = seq // tm
    return pl.pallas_call(
        _norm_kernel,
        grid=(m // tm,),
        in_specs=[
            pl.BlockSpec((tm, d), lambda i: (i, 0)),
            pl.BlockSpec((None, N_MOD, d), lambda i: (i // tiles_per_seq, 0, 0)),
            pl.BlockSpec((1, d), lambda i: (0, 0)),
        ],
        out_specs=pl.BlockSpec((tm, d), lambda i: (i, 0)),
        out_shape=jax.ShapeDtypeStruct((m, d), BF16),
        compiler_params=pltpu.CompilerParams(
            dimension_semantics=("parallel",),
            vmem_limit_bytes=V7X_VMEM_LIMIT),
        name="hgrn_norm",
    )(x2, mod_l, norm_gain_l.reshape(1, d))


def _level_split(l):
    n = REC_CHUNK >> l
    return n, n // 2


def _rec_constants():
    c = REC_CHUNK
    t = np.arange(c)[:, None]
    u = np.arange(c)[None, :]
    mats = [u <= t]
    level = np.full((c, c), -1, np.int32)
    for l in range(REC_LEVELS):
        n, half = _level_split(l)
        mid = (t // n) * n + half
        if l in REC_FINE_LEVELS:
            mats.append(np.where(t >= mid, (u >= mid) & (u <= t), (u > t) & (u < mid)))
        same = (t // n) == (u // n)
        level[same & (t % n >= half) & (u % n < half)] = l
    level[np.arange(c), np.arange(c)] = REC_LEVELS
    return np.concatenate(mats, axis=0).astype(np.float32), level


def _coarse_level_operand(q, k, cum, l):
    n, half = _level_split(l)
    pieces = []
    for lo in range(0, REC_CHUNK, n):
        mid, hi = lo + half, lo + n
        ref = cum[mid - 1:mid]
        pieces.append(k[lo:mid] * jnp.exp(ref - cum[lo:mid]))
        pieces.append(q[mid:hi] * jnp.exp(cum[mid:hi] - ref))
    return jnp.concatenate(pieces, axis=0)


def _rec_chunk(proj_ref, rows, lb_all, gain_ref, sums_ref, lvl_ref, o_ref, st_ref, heads):
    c = REC_CHUNK
    lvl = lvl_ref[...]
    in_level = [lvl == l for l in range(REC_LEVELS + 1)]
    rowi = lax.broadcasted_iota(jnp.int32, (c, HEAD_DIM), 0)
    second_half = {l: (rowi & _level_split(l)[1]) != 0 for l in REC_FINE_LEVELS + (REC_LEVELS - 1,)}
    sums = sums_ref[...]
    for j in range(heads):
        lanes = slice(j * HEAD_DIM, (j + 1) * HEAD_DIM)
        q = proj_ref[0, rows, lanes]
        v = proj_ref[2, rows, lanes].astype(BF16)
        gate = proj_ref[3, rows, lanes]
        lb = lb_all[:, lanes]
        fgt = lb + (1.0 - lb) * _sigmoid(proj_ref[1, rows, lanes])
        k = 1.0 - fgt
        logf = jnp.log(fgt)
        hi = logf.astype(BF16)
        lo = (logf - hi.astype(F32)).astype(BF16)
        part = _dot(sums, jnp.concatenate([hi, lo], axis=1))
        ex = part[:, :HEAD_DIM] + part[:, HEAD_DIM:]
        cum = ex[:c]

        scores = jnp.zeros((c, c), F32)
        for l in range(REC_LEVELS):
            if l in REC_FINE_LEVELS:
                i = 1 + REC_FINE_LEVELS.index(l)
                xl = jnp.where(second_half[l], q, k) * jnp.exp(ex[i * c:(i + 1) * c])
            elif l == REC_LEVELS - 1:
                xl = jnp.where(second_half[l], q * fgt, k)
            else:
                xl = _coarse_level_operand(q, k, cum, l)
            xl = xl.astype(BF16)
            scores = jnp.where(in_level[l], _dot_nt(xl, xl), scores)
        scores = jnp.where(in_level[REC_LEVELS], _dot_nt(q.astype(BF16), k.astype(BF16)), scores)

        st = st_ref[j]
        end = cum[c - 1:c]
        qd = (q * jnp.exp(cum)).astype(BF16)
        o = _dot(scores.astype(BF16), v) + _dot_nt(qd, st.astype(BF16))
        kd = (k * jnp.exp(end - cum)).astype(BF16)
        st_ref[j] = st * jnp.exp(end) + _dot_tn(v, kd)

        ms = jnp.mean(o * o, axis=-1, keepdims=True)
        on = o * lax.rsqrt(ms + EPS) * gain_ref[:, lanes] * (gate * _sigmoid(gate))
        o_ref[rows, lanes] = on.astype(BF16)


def _hgrn_kernel(h0_ref, hn_ref, w_ref, lbp_ref, gain_ref, sums_ref, lvl_ref, o_ref,
                 pa_ref, pb_ref, st_ref, *, layer, heads, steps_per_seq):
    n = pl.program_id(0)

    @pl.when(n == 0)
    def _():
        for part in range(N_PROJ):
            pa_ref[part] = _dot(h0_ref[...], w_ref[part])

    @pl.when(n % steps_per_seq == 0)
    def _():
        st_ref[...] = jnp.zeros_like(st_ref)

    lbp = lbp_ref[...]
    e = jnp.exp(lbp - jnp.max(lbp, axis=0, keepdims=True))
    p = e / jnp.sum(e, axis=0, keepdims=True)
    lb_all = jnp.sum(p[1:layer + 1], axis=0, keepdims=True) if layer > 0 else jnp.zeros_like(p[0:1])

    def run(cur_ref, nxt_ref):
        def chunk(s, carry):
            nxt_ref[s] = _dot(hn_ref[...], w_ref[s])
            rows = pl.ds(pl.multiple_of(s * REC_CHUNK, REC_CHUNK), REC_CHUNK)
            _rec_chunk(cur_ref, rows, lb_all, gain_ref, sums_ref, lvl_ref, o_ref, st_ref, heads)
            return carry
        lax.fori_loop(0, N_PROJ, chunk, 0)

    @pl.when(n % 2 == 0)
    def _():
        run(pa_ref, pb_ref)

    @pl.when(n % 2 == 1)
    def _():
        run(pb_ref, pa_ref)


def _hgrn_mixer(h, w_in, hgrn_lb, out_norm_gain_l, *, layer_idx, batch, seq, layer):
    m, d = h.shape
    depth = hgrn_lb.shape[0]
    groups, wl = w_in.shape[1], w_in.shape[4]
    hb = wl // HEAD_DIM
    rb = N_PROJ * REC_CHUNK
    assert seq % rb == 0
    steps_per_seq = seq // rb
    steps_per_group = batch * steps_per_seq
    n_steps = groups * steps_per_group
    sums_np, lvl_np = _rec_constants()
    sums = jnp.asarray(sums_np, BF16)
    lvl = jnp.asarray(lvl_np)
    nxt = lambda n: jnp.minimum(n + 1, n_steps - 1)

    return pl.pallas_call(
        functools.partial(_hgrn_kernel, layer=layer, heads=hb, steps_per_seq=steps_per_seq),
        grid=(n_steps,),
        in_specs=[
            pl.BlockSpec((rb, d), lambda n: (0, 0)),
            pl.BlockSpec((rb, d), lambda n: (nxt(n) % steps_per_group, 0)),
            pl.BlockSpec((None, None, N_PROJ, d, wl), lambda n: (layer_idx, nxt(n) // steps_per_group, 0, 0, 0),
                         pipeline_mode=pl.Buffered(1)),
            pl.BlockSpec((depth, wl), lambda n: (0, n // steps_per_group)),
            pl.BlockSpec((1, wl), lambda n: (0, n // steps_per_group)),
            pl.BlockSpec(sums.shape, lambda n: (0, 0)),
            pl.BlockSpec(lvl.shape, lambda n: (0, 0)),
        ],
        out_specs=pl.BlockSpec((rb, wl), lambda n: (n % steps_per_group, n // steps_per_group)),
        out_shape=jax.ShapeDtypeStruct((m, d), BF16),
        scratch_shapes=[pltpu.VMEM((N_PROJ, rb, wl), F32), pltpu.VMEM((N_PROJ, rb, wl), F32),
                        pltpu.VMEM((hb, HEAD_DIM, HEAD_DIM), F32)],
        compiler_params=pltpu.CompilerParams(
            dimension_semantics=("arbitrary",),
            vmem_limit_bytes=V7X_VMEM_LIMIT),
        name="hgrn_proj_recurrence",
    )(h, h, w_in, hgrn_lb, out_norm_gain_l.reshape(1, d), sums, lvl)


def kernel(x, c, norm_mix_gain, norm_ffn_gain, w_ada, b_ada, pool_w, pool_scale, hgrn_w_in, hgrn_w_out,
           hgrn_norm_gain, hgrn_lb, w_ffn_in, w_ffn_out, final_gain):
    batch, seq, d = x.shape
    depth = w_ada.shape[0]
    mods = _modulation(c, w_ada, b_ada).reshape(depth, batch, N_MOD, d)
    x2 = x.reshape(batch * seq, d)
    w_in16, w_out16 = w_ffn_in.astype(BF16), w_ffn_out.astype(BF16)
    pool_w16, hgrn_out16 = pool_w.astype(BF16), hgrn_w_out.astype(BF16)
    n_heads = d // HEAD_DIM
    hb = min(REC_HEADS, n_heads)
    hgrn_in16 = hgrn_w_in.astype(BF16).reshape(-1, d, N_PROJ, n_heads // hb, hb * HEAD_DIM)
    hgrn_in16 = hgrn_in16.transpose(0, 3, 2, 1, 4)
    for layer in range(depth):
        j = layer // 2
        fin = final_gain if layer == depth - 1 else None
        if layer % 2 == 0:
            mixer_args = (norm_mix_gain[layer], pool_w16, pool_scale[j], j)
            x2 = _ffn_layer(x2, mods[layer], norm_ffn_gain[layer], w_in16, w_out16, layer=layer, seq=seq,
                            mixer="pool", mixer_args=mixer_args, final_gain=fin)
        else:
            h = _hgrn_norm(x2, mods[layer], norm_mix_gain[layer], seq=seq)
            og = _hgrn_mixer(h, hgrn_in16, hgrn_lb, hgrn_norm_gain[j], layer_idx=j, batch=batch, seq=seq,
                             layer=layer)
            mixer_args = (og, hgrn_out16, j)
            x2 = _ffn_layer(x2, mods[layer], norm_ffn_gain[layer], w_in16, w_out16, layer=layer, seq=seq,
                            mixer="hgrn", mixer_args=mixer_args, final_gain=fin)
    return x2.reshape(batch, seq, d)
```

```python
import functools

import numpy as np
import jax
import jax.numpy as jnp
from jax import lax
from jax.experimental import pallas as pl
from jax.experimental.pallas import tpu as pltpu

EPS = 1e-6
LOG2_E = 1.4426950408889634
POOL_WINDOWS = (2, 4, 8, 16)
POOL_HALO = 16
HEAD_DIM = 128
N_MOD = 6
N_PROJ = 4
REC_HEADS = 8
REC_CHUNK = 128
REC_LEVELS = 7
REC_FINE_LEVELS = (4, 5)
MXU_WIDTH = 256
V7X_VMEM_LIMIT = 56 * 1024 * 1024

F32 = jnp.float32
BF16 = jnp.bfloat16


def _sigmoid(x):
    return 1.0 / (1.0 + jnp.exp(-x))


def _mod_norm(x, gain, shift, scale):
    ms = jnp.mean(x * x, axis=-1, keepdims=True)
    return x * lax.rsqrt(ms + EPS) * (gain * (1.0 + scale)) + shift


def _dot(a, b):
    return jnp.dot(a, b, preferred_element_type=F32)


def _dot_nt(a, b):
    return lax.dot_general(a, b, (((1,), (1,)), ((), ())), preferred_element_type=F32)


def _dot_tn(a, b):
    return lax.dot_general(a, b, (((0,), (0,)), ((), ())), preferred_element_type=F32)


def _mod_kernel(c_ref, w_ref, b_ref, o_ref, *, batch):
    cpad = c_ref[...]
    cond = (cpad * _sigmoid(cpad)).astype(BF16)
    res = _dot(cond, w_ref[...].astype(BF16)) + b_ref[...]
    o_ref[...] = res[:batch]


def _modulation(c, w_ada, b_ada):
    depth, d, n = w_ada.shape
    batch = c.shape[0]
    rows = 16
    cpad = jnp.zeros((rows, d), F32).at[:batch].set(c)
    tn = min(n, 1024)
    return pl.pallas_call(
        functools.partial(_mod_kernel, batch=batch),
        grid=(depth, n // tn),
        in_specs=[
            pl.BlockSpec((rows, d), lambda l, j: (0, 0)),
            pl.BlockSpec((None, d, tn), lambda l, j: (l, 0, j)),
            pl.BlockSpec((None, 1, tn), lambda l, j: (l, 0, j)),
        ],
        out_specs=pl.BlockSpec((None, batch, tn), lambda l, j: (l, 0, j)),
        out_shape=jax.ShapeDtypeStruct((depth, batch, n), F32),
        compiler_params=pltpu.CompilerParams(
            dimension_semantics=("parallel", "parallel"),
            vmem_limit_bytes=V7X_VMEM_LIMIT),
        name="adaln_mod",
    )(cpad, w_ada, b_ada.reshape(depth, 1, n))


def _pool_mixer_tile(x, xh_ref, gain, shift, scale, pw_ref, ps_ref, hs_ref, first_tile, pos_base):
    tm, d = x.shape
    g_dim = d // len(POOL_WINDOWS)
    h = _mod_norm(x, gain, shift, scale)
    hh = _mod_norm(xh_ref[...], gain, shift, scale)
    hs_ref[0:POOL_HALO, :] = jnp.where(first_tile, 0.0, hh)
    hs_ref[POOL_HALO:, :] = h
    row = lax.broadcasted_iota(jnp.int32, (POOL_HALO, g_dim), 0)
    pos = (pos_base + row + 1).astype(F32)
    ys = []
    for g, w in enumerate(POOL_WINDOWS):
        lanes = slice(g * g_dim, (g + 1) * g_dim)
        s = hs_ref[:, lanes]
        span = 1
        while span < w:
            s = s + pltpu.roll(s, span, axis=0)
            span *= 2
        s = s[POOL_HALO:]
        hg = hs_ref[POOL_HALO:, lanes]
        top = s[:POOL_HALO] / jnp.minimum(pos, float(w)) - hg[:POOL_HALO]
        rest = s[POOL_HALO:] * (1.0 / w) - hg[POOL_HALO:]
        dg = jnp.concatenate([top, rest], axis=0).astype(BF16)
        ys.append(_dot(dg, pw_ref[g]))
    return jnp.concatenate(ys, axis=1) * ps_ref[...]


def _ffn_kernel(*refs, mixer, final, feeds_next, nf, tiles_per_seq, tm):
    if mixer == "pool":
        (x_ref, xh_ref, mod_ref, gm_ref, gf_ref, pw_ref, ps_ref, wa_ref, wb_ref, wo_ref) = refs[:10]
        rest = refs[10:]
    else:
        (x_ref, og_ref, mod_ref, gf_ref, wout_ref, wa_ref, wb_ref, wo_ref) = refs[:8]
        rest = refs[8:]
    if final:
        fg_ref, rest = rest[0], rest[1:]
    if feeds_next:
        modn_ref, gn_ref, rest = rest[0], rest[1], rest[2:]
        o_ref, hn_ref, rest = rest[0], rest[1], rest[2:]
    else:
        o_ref, rest = rest[0], rest[1:]
    if mixer == "pool":
        h2_ref, acc_ref, hs_ref = rest
    else:
        h2_ref, acc_ref = rest

    i = pl.program_id(0)
    f = pl.program_id(1)
    tile_in_seq = i % tiles_per_seq

    @pl.when(f == 0)
    def _():
        x = x_ref[...]
        if mixer == "pool":
            y = _pool_mixer_tile(x, xh_ref, gm_ref[...], mod_ref[0:1, :], mod_ref[1:2, :],
                                 pw_ref, ps_ref, hs_ref, tile_in_seq == 0, tile_in_seq * tm)
        else:
            y = _dot(og_ref[...], wout_ref[...])
        xm = x + mod_ref[2:3, :] * y
        o_ref[...] = xm
        h2 = _mod_norm(xm, gf_ref[...], mod_ref[3:4, :], mod_ref[4:5, :])
        h2_ref[...] = h2.astype(BF16)
        acc_ref[...] = jnp.zeros_like(acc_ref)

    h2 = h2_ref[...]
    a = _dot(h2, wa_ref[...])
    b = _dot(h2, wb_ref[...])
    u = (a * _sigmoid(a) * b).astype(BF16)
    acc_ref[...] += _dot(u, wo_ref[...])

    @pl.when(f == nf - 1)
    def _():
        out = o_ref[...] + mod_ref[5:6, :] * acc_ref[...]
        if final:
            ms = jnp.mean(out * out, axis=-1, keepdims=True)
            out = out * lax.rsqrt(ms + EPS) * fg_ref[...]
        o_ref[...] = out
        if feeds_next:
            hn = _mod_norm(out, gn_ref[...], modn_ref[0:1, :], modn_ref[1:2, :])
            hn_ref[...] = hn.astype(BF16)


def _ffn_layer(x2, mod_l, norm_ffn_gain_l, w_in, w_out, *, layer, seq, mixer, mixer_args, final_gain=None,
               next_mixer=None, tm=512, tf=512):
    m, d = x2.shape
    d_ff = w_out.shape[1]
    tm = min(tm, seq)
    tf = min(tf, d_ff)
    assert seq % tm == 0 and d_ff % tf == 0 and tm % POOL_HALO == 0
    nf = d_ff // tf
    tiles_per_seq = seq // tm
    row = lambda v: v.reshape(1, d)
    x_spec = pl.BlockSpec((tm, d), lambda i, f: (i, 0))
    mod_spec = pl.BlockSpec((None, N_MOD, d), lambda i, f: (i // tiles_per_seq, 0, 0))
    vec_spec = pl.BlockSpec((1, d), lambda i, f: (0, 0))
    ffn_specs = [
        pl.BlockSpec((None, d, tf), lambda i, f: (layer, 0, f)),
        pl.BlockSpec((None, d, tf), lambda i, f: (layer, 0, nf + f)),
        pl.BlockSpec((None, tf, d), lambda i, f: (layer, f, 0)),
    ]
    scratch = [pltpu.VMEM((tm, d), BF16), pltpu.VMEM((tm, d), F32)]
    if mixer == "pool":
        norm_mix_gain_l, pool_w, pool_scale, j = mixer_args
        _, n_groups, g_dim, _ = pool_w.shape
        halo_blocks = tm // POOL_HALO
        args = [x2, x2, mod_l, row(norm_mix_gain_l), row(norm_ffn_gain_l), pool_w, row(pool_scale)]
        specs = [x_spec,
                 pl.BlockSpec((POOL_HALO, d), lambda i, f: (jnp.maximum(i * halo_blocks - 1, 0), 0)),
                 mod_spec, vec_spec, vec_spec,
                 pl.BlockSpec((None, n_groups, g_dim, g_dim), lambda i, f: (j, 0, 0, 0)),
                 vec_spec]
        scratch.append(pltpu.VMEM((tm + POOL_HALO, d), F32))
    else:
        og, w_mix_out, j = mixer_args
        args = [x2, og, mod_l, row(norm_ffn_gain_l), w_mix_out]
        specs = [x_spec, pl.BlockSpec((tm, d), lambda i, f: (i, 0)), mod_spec, vec_spec,
                 pl.BlockSpec((None, d, d), lambda i, f: (j, 0, 0), pipeline_mode=pl.Buffered(1))]
    args += [w_in, w_in, w_out]
    specs += ffn_specs
    if final_gain is not None:
        args.append(row(final_gain))
        specs.append(vec_spec)
    out_specs = pl.BlockSpec((tm, d), lambda i, f: (i, 0))
    out_shape = jax.ShapeDtypeStruct((m, d), F32)
    if next_mixer is not None:
        args += [next_mixer[0], row(next_mixer[1])]
        specs += [mod_spec, vec_spec]
        out_specs = [out_specs, pl.BlockSpec((tm, d), lambda i, f: (i, 0))]
        out_shape = [out_shape, jax.ShapeDtypeStruct((m, d), BF16)]
    return pl.pallas_call(
        functools.partial(_ffn_kernel, mixer=mixer, final=final_gain is not None,
                          feeds_next=next_mixer is not None, nf=nf, tiles_per_seq=tiles_per_seq, tm=tm),
        grid=(m // tm, nf),
        in_specs=specs,
        out_specs=out_specs,
        out_shape=out_shape,
        scratch_shapes=scratch,
        compiler_params=pltpu.CompilerParams(
            dimension_semantics=("parallel", "arbitrary"),
            vmem_limit_bytes=V7X_VMEM_LIMIT),
        name="mixer_out_ffn_" + mixer,
    )(*args)


def _level_split(l):
    n = REC_CHUNK >> l
    return n, n // 2


def _rec_constants():
    c = REC_CHUNK
    t = np.arange(c)[:, None]
    u = np.arange(c)[None, :]
    mats = [u <= t]
    level = np.full((c, c), -1, np.int32)
    for l in range(REC_LEVELS):
        n, half = _level_split(l)
        mid = (t // n) * n + half
        if l in REC_FINE_LEVELS:
            mats.append(np.where(t >= mid, (u >= mid) & (u <= t), (u > t) & (u < mid)))
        same = (t // n) == (u // n)
        level[same & (t % n >= half) & (u % n < half)] = l
    level[np.arange(c), np.arange(c)] = REC_LEVELS
    return np.concatenate(mats, axis=0).astype(np.float32), level


def _coarse_level_operand(q, k, cum, l):
    n, half = _level_split(l)
    pieces = []
    for lo in range(0, REC_CHUNK, n):
        mid, hi = lo + half, lo + n
        ref = cum[mid - 1:mid]
        pieces.append(k[lo:mid] * jnp.exp2(ref - cum[lo:mid]))
        pieces.append(q[mid:hi] * jnp.exp2(cum[mid:hi] - ref))
    return jnp.concatenate(pieces, axis=0)


def _rec_chunk(proj_ref, rows, lb_all, gain_ref, sums_ref, lvl_ref, o_ref, st_ref, heads, after_head=None):
    c = REC_CHUNK
    lvl = lvl_ref[...]
    in_level = [lvl == l for l in range(REC_LEVELS + 1)]
    rowi = lax.broadcasted_iota(jnp.int32, (c, HEAD_DIM), 0)
    second_half = {l: (rowi & _level_split(l)[1]) != 0 for l in REC_FINE_LEVELS + (REC_LEVELS - 1,)}
    sums = sums_ref[...]
    hook = iter(range(3 * heads + 1))

    def fill():
        if after_head is not None:
            after_head(next(hook))

    fill()

    gates, parts = [], []
    for j in range(heads):
        lanes = slice(j * HEAD_DIM, (j + 1) * HEAD_DIM)
        lb = lb_all[:, lanes]
        fgt = lb + (1.0 - lb) * _sigmoid(proj_ref[1, rows, lanes])
        logf = jnp.log(fgt) * LOG2_E
        hi = logf.astype(BF16)
        lo = (logf - hi.astype(F32)).astype(BF16)
        gates.append(fgt)
        parts.append(_dot(sums, jnp.concatenate([hi, lo], axis=1)))
        fill()

    all_scores, cums = [], []
    for j in range(heads):
        lanes = slice(j * HEAD_DIM, (j + 1) * HEAD_DIM)
        q = proj_ref[0, rows, lanes]
        fgt = gates[j]
        k = 1.0 - fgt
        ex = parts[j][:, :HEAD_DIM] + parts[j][:, HEAD_DIM:]
        cum = ex[:c]
        scores = jnp.zeros((c, c), F32)
        for l in range(REC_LEVELS):
            if l in REC_FINE_LEVELS:
                i = 1 + REC_FINE_LEVELS.index(l)
                xl = jnp.where(second_half[l], q, k) * jnp.exp2(ex[i * c:(i + 1) * c])
            elif l == REC_LEVELS - 1:
                xl = jnp.where(second_half[l], q * fgt, k)
            else:
                xl = _coarse_level_operand(q, k, cum, l)
            xl = xl.astype(BF16)
            scores = jnp.where(in_level[l], _dot_nt(xl, xl), scores)
        scores = jnp.where(in_level[REC_LEVELS], jnp.sum(q * k, axis=-1, keepdims=True), scores)
        all_scores.append(scores.astype(BF16))
        cums.append(cum)
        fill()

    for j in range(heads):
        lanes = slice(j * HEAD_DIM, (j + 1) * HEAD_DIM)
        q = proj_ref[0, rows, lanes]
        v = proj_ref[2, rows, lanes].astype(BF16)
        gate = proj_ref[3, rows, lanes]
        k = 1.0 - gates[j]
        cum = cums[j]
        st = st_ref[j]
        end = cum[c - 1:c]
        qd = (q * jnp.exp2(cum)).astype(BF16)
        o = _dot(all_scores[j], v) + _dot_nt(qd, st.astype(BF16))
        kd = (k * jnp.exp2(end - cum)).astype(BF16)
        st_ref[j] = st * jnp.exp2(end) + _dot_tn(v, kd)

        ms = jnp.mean(o * o, axis=-1, keepdims=True)
        on = o * lax.rsqrt(ms + EPS) * gain_ref[:, lanes] * (gate * _sigmoid(gate))
        o_ref[rows, lanes] = on.astype(BF16)
        fill()


def _hgrn_kernel(h0_ref, hn_ref, w_ref, lbp_ref, gain_ref, sums_ref, lvl_ref, o_ref,
                 pa_ref, pb_ref, st_ref, *, layer, heads, steps_per_seq):
    n = pl.program_id(0)

    @pl.when(n == 0)
    def _():
        for part in range(N_PROJ):
            pa_ref[part] = _dot(h0_ref[...], w_ref[part])

    @pl.when(n % steps_per_seq == 0)
    def _():
        st_ref[...] = jnp.zeros_like(st_ref)

    lbp = lbp_ref[...]
    e = jnp.exp(lbp - jnp.max(lbp, axis=0, keepdims=True))
    p = e / jnp.sum(e, axis=0, keepdims=True)
    lb_all = jnp.sum(p[1:layer + 1], axis=0, keepdims=True) if layer > 0 else jnp.zeros_like(p[0:1])

    wl = heads * HEAD_DIM
    col_w = min(MXU_WIDTH, wl)
    pieces = [(r, cb) for r in range(N_PROJ) for cb in range(wl // col_w)]
    assert len(pieces) <= 3 * heads

    def run(cur_ref, nxt_ref):
        def chunk(s, carry):
            def project_slice(hook_idx):
                if not 1 <= hook_idx <= len(pieces):
                    return
                r, cb = pieces[hook_idx - 1]
                rws = slice(r * REC_CHUNK, (r + 1) * REC_CHUNK)
                cols = slice(cb * col_w, (cb + 1) * col_w)
                nxt_ref[s, rws, cols] = _dot(hn_ref[rws, :], w_ref[s, :, cols])
            rows = pl.ds(pl.multiple_of(s * REC_CHUNK, REC_CHUNK), REC_CHUNK)
            _rec_chunk(cur_ref, rows, lb_all, gain_ref, sums_ref, lvl_ref, o_ref, st_ref, heads,
                       after_head=project_slice)
            return carry
        lax.fori_loop(0, N_PROJ, chunk, 0)

    @pl.when(n % 2 == 0)
    def _():
        run(pa_ref, pb_ref)

    @pl.when(n % 2 == 1)
    def _():
        run(pb_ref, pa_ref)


def _hgrn_mixer(h, w_in, hgrn_lb, out_norm_gain_l, *, layer_idx, batch, seq, layer):
    m, d = h.shape
    depth = hgrn_lb.shape[0]
    groups, wl = w_in.shape[1], w_in.shape[4]
    hb = wl // HEAD_DIM
    rb = N_PROJ * REC_CHUNK
    assert seq % rb == 0
    steps_per_seq = seq // rb
    steps_per_group = batch * steps_per_seq
    n_steps = groups * steps_per_group
    sums_np, lvl_np = _rec_constants()
    sums = jnp.asarray(sums_np, BF16)
    lvl = jnp.asarray(lvl_np)
    nxt = lambda n: jnp.minimum(n + 1, n_steps - 1)

    return pl.pallas_call(
        functools.partial(_hgrn_kernel, layer=layer, heads=hb, steps_per_seq=steps_per_seq),
        grid=(n_steps,),
        in_specs=[
            pl.BlockSpec((rb, d), lambda n: (0, 0)),
            pl.BlockSpec((rb, d), lambda n: (nxt(n) % steps_per_group, 0)),
            pl.BlockSpec((None, None, N_PROJ, d, wl), lambda n: (layer_idx, nxt(n) // steps_per_group, 0, 0, 0),
                         pipeline_mode=pl.Buffered(1)),
            pl.BlockSpec((depth, wl), lambda n: (0, n // steps_per_group)),
            pl.BlockSpec((1, wl), lambda n: (0, n // steps_per_group)),
            pl.BlockSpec(sums.shape, lambda n: (0, 0)),
            pl.BlockSpec(lvl.shape, lambda n: (0, 0)),
        ],
        out_specs=pl.BlockSpec((rb, wl), lambda n: (n % steps_per_group, n // steps_per_group)),
        out_shape=jax.ShapeDtypeStruct((m, d), BF16),
        scratch_shapes=[pltpu.VMEM((N_PROJ, rb, wl), F32), pltpu.VMEM((N_PROJ, rb, wl), F32),
                        pltpu.VMEM((hb, HEAD_DIM, HEAD_DIM), F32)],
        compiler_params=pltpu.CompilerParams(
            dimension_semantics=("arbitrary",),
            vmem_limit_bytes=V7X_VMEM_LIMIT),
        name="hgrn_proj_recurrence",
    )(h, h, w_in, hgrn_lb, out_norm_gain_l.reshape(1, d), sums, lvl)


def kernel(x, c, norm_mix_gain, norm_ffn_gain, w_ada, b_ada, pool_w, pool_scale, hgrn_w_in, hgrn_w_out,
           hgrn_norm_gain, hgrn_lb, w_ffn_in, w_ffn_out, final_gain):
    batch, seq, d = x.shape
    depth = w_ada.shape[0]
    mods = _modulation(c, w_ada, b_ada).reshape(depth, batch, N_MOD, d)
    x2 = x.reshape(batch * seq, d)
    w_in16, w_out16 = w_ffn_in.astype(BF16), w_ffn_out.astype(BF16)
    pool_w16, hgrn_out16 = pool_w.astype(BF16), hgrn_w_out.astype(BF16)
    n_heads = d // HEAD_DIM
    hb = min(REC_HEADS, n_heads)
    hgrn_in16 = hgrn_w_in.reshape(-1, d, N_PROJ, n_heads // hb, hb * HEAD_DIM)
    hgrn_in16 = hgrn_in16.transpose(0, 3, 2, 1, 4).astype(BF16)
    assert depth % 2 == 0, "layers alternate pooling / HGRN2; each pooling layer feeds the HGRN2 layer after it"
    h = None
    for layer in range(depth):
        j = layer // 2
        fin = final_gain if layer == depth - 1 else None
        if layer % 2 == 0:
            mixer_args = (norm_mix_gain[layer], pool_w16, pool_scale[j], j)
            x2, h = _ffn_layer(x2, mods[layer], norm_ffn_gain[layer], w_in16, w_out16, layer=layer, seq=seq,
                               mixer="pool", mixer_args=mixer_args, final_gain=fin,
                               next_mixer=(mods[layer + 1], norm_mix_gain[layer + 1]))
        else:
            og = _hgrn_mixer(h, hgrn_in16, hgrn_lb, hgrn_norm_gain[j], layer_idx=j, batch=batch, seq=seq,
                             layer=layer)
            mixer_args = (og, hgrn_out16, j)
            x2 = _ffn_layer(x2, mods[layer], norm_ffn_gain[layer], w_in16, w_out16, layer=layer, seq=seq,
                            mixer="hgrn", mixer_args=mixer_args, final_gain=fin)
    return x2.reshape(batch, seq, d)
```

```python
import functools

import numpy as np
import jax
import jax.numpy as jnp
from jax import lax
from jax.experimental import pallas as pl
from jax.experimental.pallas import tpu as pltpu

EPS = 1e-6
LOG2_E = 1.4426950408889634
POOL_WINDOWS = (2, 4, 8, 16)
POOL_HALO = 16
HEAD_DIM = 128
N_MOD = 6
FFN_SUBTILES = 2
N_PROJ = 4
REC_HEADS = 8
REC_CHUNK = 128
REC_LEVELS = 7
REC_FINE_LEVELS = (4, 5)
MXU_WIDTH = 256
V7X_VMEM_LIMIT = 56 * 1024 * 1024

F32 = jnp.float32
BF16 = jnp.bfloat16


def _sigmoid(x):
    return 1.0 / (1.0 + jnp.exp(-x))


def _mod_norm(x, gain, shift, scale):
    ms = jnp.mean(x * x, axis=-1, keepdims=True)
    return x * lax.rsqrt(ms + EPS) * (gain * (1.0 + scale)) + shift


def _dot(a, b):
    return jnp.dot(a, b, preferred_element_type=F32)


def _dot_nt(a, b):
    return lax.dot_general(a, b, (((1,), (1,)), ((), ())), preferred_element_type=F32)


def _dot_tn(a, b):
    return lax.dot_general(a, b, (((0,), (0,)), ((), ())), preferred_element_type=F32)


def _mod_kernel(c_ref, w_ref, b_ref, o_ref, *, batch):
    cpad = c_ref[...]
    cond = (cpad * _sigmoid(cpad)).astype(BF16)
    res = _dot(cond, w_ref[...].astype(BF16)) + b_ref[...]
    o_ref[...] = res[:batch]


def _modulation(c, w_ada, b_ada):
    depth, d, n = w_ada.shape
    batch = c.shape[0]
    rows = 16
    cpad = jnp.zeros((rows, d), F32).at[:batch].set(c)
    tn = min(n, 1024)
    return pl.pallas_call(
        functools.partial(_mod_kernel, batch=batch),
        grid=(depth, n // tn),
        in_specs=[
            pl.BlockSpec((rows, d), lambda l, j: (0, 0)),
            pl.BlockSpec((None, d, tn), lambda l, j: (l, 0, j)),
            pl.BlockSpec((None, 1, tn), lambda l, j: (l, 0, j)),
        ],
        out_specs=pl.BlockSpec((None, batch, tn), lambda l, j: (l, 0, j)),
        out_shape=jax.ShapeDtypeStruct((depth, batch, n), F32),
        compiler_params=pltpu.CompilerParams(
            dimension_semantics=("parallel", "parallel"),
            vmem_limit_bytes=V7X_VMEM_LIMIT),
        name="adaln_mod",
    )(cpad, w_ada, b_ada.reshape(depth, 1, n))


def _pool_mixer_rows(x, r0, xh_ref, gain, shift, scale, pw_ref, ps_ref, hs_ref, first_tile, pos_base):
    n, d = x.shape
    g_dim = d // len(POOL_WINDOWS)
    if r0 == 0:
        hh = _mod_norm(xh_ref[...], gain, shift, scale)
        hs_ref[0:POOL_HALO, :] = jnp.where(first_tile, 0.0, hh)
    hs_ref[POOL_HALO + r0:POOL_HALO + r0 + n, :] = _mod_norm(x, gain, shift, scale)
    row = lax.broadcasted_iota(jnp.int32, (POOL_HALO, g_dim), 0)
    pos = (pos_base + row + 1).astype(F32)
    ys = []
    for g, w in enumerate(POOL_WINDOWS):
        lanes = slice(g * g_dim, (g + 1) * g_dim)
        s = hs_ref[r0:POOL_HALO + r0 + n, lanes]
        span = 1
        while span < w:
            s = s + pltpu.roll(s, span, axis=0)
            span *= 2
        s = s[POOL_HALO:]
        hg = hs_ref[POOL_HALO + r0:POOL_HALO + r0 + n, lanes]
        if r0 == 0:
            top = s[:POOL_HALO] / jnp.minimum(pos, float(w)) - hg[:POOL_HALO]
            rest = s[POOL_HALO:] * (1.0 / w) - hg[POOL_HALO:]
            dg = jnp.concatenate([top, rest], axis=0)
        else:
            dg = s * (1.0 / w) - hg
        ys.append(_dot(dg.astype(BF16), pw_ref[g]))
    return jnp.concatenate(ys, axis=1) * ps_ref[...]


def _ffn_kernel(*refs, mixer, final, feeds_next, nf, tiles_per_seq, tm):
    if mixer == "pool":
        (x_ref, xh_ref, mod_ref, gm_ref, gf_ref, pw_ref, ps_ref, wa_ref, wb_ref, wo_ref) = refs[:10]
        rest = refs[10:]
    else:
        (x_ref, og_ref, mod_ref, gf_ref, wout_ref, wa_ref, wb_ref, wo_ref) = refs[:8]
        rest = refs[8:]
    if final:
        fg_ref, rest = rest[0], rest[1:]
    if feeds_next:
        modn_ref, gn_ref, rest = rest[0], rest[1], rest[2:]
        o_ref, hn_ref, rest = rest[0], rest[1], rest[2:]
    else:
        o_ref, rest = rest[0], rest[1:]
    if mixer == "pool":
        h2_ref, acc_ref, hs_ref = rest
    else:
        h2_ref, acc_ref = rest

    i = pl.program_id(0)
    f = pl.program_id(1)
    tile_in_seq = i % tiles_per_seq

    def ffn_chunk(h2):
        a = _dot(h2, wa_ref[...])
        b = _dot(h2, wb_ref[...])
        u = (a * _sigmoid(a) * b).astype(BF16)
        return _dot(u, wo_ref[...])

    @pl.when(f == 0)
    def _():
        sub = tm // FFN_SUBTILES
        if mixer == "hgrn":
            ys = [_dot(og_ref[r0:r0 + sub, :], wout_ref[...]) for r0 in range(0, tm, sub)]
        for t, r0 in enumerate(range(0, tm, sub)):
            rows = slice(r0, r0 + sub)
            x = x_ref[rows, :]
            if mixer == "pool":
                y = _pool_mixer_rows(x, r0, xh_ref, gm_ref[...], mod_ref[0:1, :], mod_ref[1:2, :],
                                     pw_ref, ps_ref, hs_ref, tile_in_seq == 0, tile_in_seq * tm)
            else:
                y = ys[t]
            xm = x + mod_ref[2:3, :] * y
            o_ref[rows, :] = xm
            h2 = _mod_norm(xm, gf_ref[...], mod_ref[3:4, :], mod_ref[4:5, :]).astype(BF16)
            h2_ref[rows, :] = h2
            acc_ref[rows, :] = ffn_chunk(h2)

    @pl.when(jnp.logical_and(f > 0, f < nf - 1))
    def _():
        acc_ref[...] += ffn_chunk(h2_ref[...])

    @pl.when(f == nf - 1)
    def _():
        sub = tm // FFN_SUBTILES
        for r0 in range(0, tm, sub):
            rows = slice(r0, r0 + sub)
            acc = acc_ref[rows, :] + ffn_chunk(h2_ref[rows, :])
            out = o_ref[rows, :] + mod_ref[5:6, :] * acc
            if final:
                ms = jnp.mean(out * out, axis=-1, keepdims=True)
                out = out * lax.rsqrt(ms + EPS) * fg_ref[...]
            o_ref[rows, :] = out
            if feeds_next:
                hn = _mod_norm(out, gn_ref[...], modn_ref[0:1, :], modn_ref[1:2, :])
                hn_ref[rows, :] = hn.astype(BF16)


def _ffn_layer(x2, mod_l, norm_ffn_gain_l, w_in, w_out, *, layer, seq, mixer, mixer_args, final_gain=None,
               next_mixer=None, tm=512, tf=512):
    m, d = x2.shape
    d_ff = w_out.shape[1]
    tm = min(tm, seq)
    tf = min(tf, d_ff)
    assert seq % tm == 0 and d_ff % tf == 0 and tm % (FFN_SUBTILES * POOL_HALO) == 0
    nf = d_ff // tf
    assert nf >= 2, "the first and the last d_ff step are distinct code paths"
    tiles_per_seq = seq // tm
    row = lambda v: v.reshape(1, d)
    x_spec = pl.BlockSpec((tm, d), lambda i, f: (i, 0))
    mod_spec = pl.BlockSpec((None, N_MOD, d), lambda i, f: (i // tiles_per_seq, 0, 0))
    vec_spec = pl.BlockSpec((1, d), lambda i, f: (0, 0))
    ffn_specs = [
        pl.BlockSpec((None, d, tf), lambda i, f: (layer, 0, f)),
        pl.BlockSpec((None, d, tf), lambda i, f: (layer, 0, nf + f)),
        pl.BlockSpec((None, tf, d), lambda i, f: (layer, f, 0)),
    ]
    scratch = [pltpu.VMEM((tm, d), BF16), pltpu.VMEM((tm, d), F32)]
    if mixer == "pool":
        norm_mix_gain_l, pool_w, pool_scale, j = mixer_args
        _, n_groups, g_dim, _ = pool_w.shape
        halo_blocks = tm // POOL_HALO
        args = [x2, x2, mod_l, row(norm_mix_gain_l), row(norm_ffn_gain_l), pool_w, row(pool_scale)]
        specs = [x_spec,
                 pl.BlockSpec((POOL_HALO, d), lambda i, f: (jnp.maximum(i * halo_blocks - 1, 0), 0)),
                 mod_spec, vec_spec, vec_spec,
                 pl.BlockSpec((None, n_groups, g_dim, g_dim), lambda i, f: (j, 0, 0, 0)),
                 vec_spec]
        scratch.append(pltpu.VMEM((tm + POOL_HALO, d), F32))
    else:
        og, w_mix_out, j = mixer_args
        args = [x2, og, mod_l, row(norm_ffn_gain_l), w_mix_out]
        specs = [x_spec, pl.BlockSpec((tm, d), lambda i, f: (i, 0)), mod_spec, vec_spec,
                 pl.BlockSpec((None, d, d), lambda i, f: (j, 0, 0), pipeline_mode=pl.Buffered(1))]
    args += [w_in, w_in, w_out]
    specs += ffn_specs
    if final_gain is not None:
        args.append(row(final_gain))
        specs.append(vec_spec)
    out_specs = pl.BlockSpec((tm, d), lambda i, f: (i, 0))
    out_shape = jax.ShapeDtypeStruct((m, d), F32)
    if next_mixer is not None:
        args += [next_mixer[0], row(next_mixer[1])]
        specs += [mod_spec, vec_spec]
        out_specs = [out_specs, pl.BlockSpec((tm, d), lambda i, f: (i, 0))]
        out_shape = [out_shape, jax.ShapeDtypeStruct((m, d), BF16)]
    return pl.pallas_call(
        functools.partial(_ffn_kernel, mixer=mixer, final=final_gain is not None,
                          feeds_next=next_mixer is not None, nf=nf, tiles_per_seq=tiles_per_seq, tm=tm),
        grid=(m // tm, nf),
        in_specs=specs,
        out_specs=out_specs,
        out_shape=out_shape,
        scratch_shapes=scratch,
        compiler_params=pltpu.CompilerParams(
            dimension_semantics=("parallel", "arbitrary"),
            vmem_limit_bytes=V7X_VMEM_LIMIT),
        name="mixer_out_ffn_" + mixer,
    )(*args)


def _level_split(l):
    n = REC_CHUNK >> l
    return n, n // 2


def _rec_constants():
    c = REC_CHUNK
    t = np.arange(c)[:, None]
    u = np.arange(c)[None, :]
    mats = [u <= t]
    level = np.full((c, c), -1, np.int32)
    for l in range(REC_LEVELS):
        n, half = _level_split(l)
        mid = (t // n) * n + half
        if l in REC_FINE_LEVELS:
            mats.append(np.where(t >= mid, (u >= mid) & (u <= t), (u > t) & (u < mid)))
        same = (t // n) == (u // n)
        level[same & (t % n >= half) & (u % n < half)] = l
    level[np.arange(c), np.arange(c)] = REC_LEVELS
    return np.concatenate(mats, axis=0).astype(np.float32), level


def _coarse_level_operand(q, k, cum, l):
    n, half = _level_split(l)
    pieces = []
    for lo in range(0, REC_CHUNK, n):
        mid, hi = lo + half, lo + n
        ref = cum[mid - 1:mid]
        pieces.append(k[lo:mid] * jnp.exp2(ref - cum[lo:mid]))
        pieces.append(q[mid:hi] * jnp.exp2(cum[mid:hi] - ref))
    return jnp.concatenate(pieces, axis=0)


def _rec_chunk(proj_ref, rows, lb_all, gain_ref, sums_ref, lvl_ref, o_ref, st_ref, heads, after_head=None):
    c = REC_CHUNK
    lvl = lvl_ref[...]
    in_level = [lvl == l for l in range(REC_LEVELS + 1)]
    rowi = lax.broadcasted_iota(jnp.int32, (c, HEAD_DIM), 0)
    second_half = {l: (rowi & _level_split(l)[1]) != 0 for l in REC_FINE_LEVELS + (REC_LEVELS - 1,)}
    sums = sums_ref[...]
    hook = iter(range(3 * heads + 1))

    def fill():
        if after_head is not None:
            after_head(next(hook))

    fill()

    gates, parts = [], []
    for j in range(heads):
        lanes = slice(j * HEAD_DIM, (j + 1) * HEAD_DIM)
        lb = lb_all[:, lanes]
        fgt = lb + (1.0 - lb) * _sigmoid(proj_ref[1, rows, lanes])
        logf = jnp.log(fgt) * LOG2_E
        hi = logf.astype(BF16)
        lo = (logf - hi.astype(F32)).astype(BF16)
        gates.append(fgt)
        parts.append(_dot(sums, jnp.concatenate([hi, lo], axis=1)))
        fill()

    all_scores, cums = [], []
    for j in range(heads):
        lanes = slice(j * HEAD_DIM, (j + 1) * HEAD_DIM)
        q = proj_ref[0, rows, lanes]
        fgt = gates[j]
        k = 1.0 - fgt
        ex = parts[j][:, :HEAD_DIM] + parts[j][:, HEAD_DIM:]
        cum = ex[:c]
        scores = jnp.zeros((c, c), F32)
        for l in range(REC_LEVELS):
            if l in REC_FINE_LEVELS:
                i = 1 + REC_FINE_LEVELS.index(l)
                xl = jnp.where(second_half[l], q, k) * jnp.exp2(ex[i * c:(i + 1) * c])
            elif l == REC_LEVELS - 1:
                xl = jnp.where(second_half[l], q * fgt, k)
            else:
                xl = _coarse_level_operand(q, k, cum, l)
            xl = xl.astype(BF16)
            scores = jnp.where(in_level[l], _dot_nt(xl, xl), scores)
        scores = jnp.where(in_level[REC_LEVELS], jnp.sum(q * k, axis=-1, keepdims=True), scores)
        all_scores.append(scores.astype(BF16))
        cums.append(cum)
        fill()

    for j in range(heads):
        lanes = slice(j * HEAD_DIM, (j + 1) * HEAD_DIM)
        q = proj_ref[0, rows, lanes]
        v = proj_ref[2, rows, lanes].astype(BF16)
        gate = proj_ref[3, rows, lanes]
        k = 1.0 - gates[j]
        cum = cums[j]
        st = st_ref[j]
        end = cum[c - 1:c]
        qd = (q * jnp.exp2(cum)).astype(BF16)
        o = _dot(all_scores[j], v) + _dot_nt(qd, st.astype(BF16))
        kd = (k * jnp.exp2(end - cum)).astype(BF16)
        st_ref[j] = st * jnp.exp2(end) + _dot_tn(v, kd)

        ms = jnp.mean(o * o, axis=-1, keepdims=True)
        on = o * lax.rsqrt(ms + EPS) * gain_ref[:, lanes] * (gate * _sigmoid(gate))
        o_ref[rows, lanes] = on.astype(BF16)
        fill()


def _hgrn_kernel(h0_ref, hn_ref, w_ref, lbp_ref, gain_ref, sums_ref, lvl_ref, o_ref,
                 pa_ref, pb_ref, st_ref, *, layer, heads, steps_per_seq):
    n = pl.program_id(0)

    @pl.when(n == 0)
    def _():
        for part in range(N_PROJ):
            pa_ref[part] = _dot(h0_ref[...], w_ref[part])

    @pl.when(n % steps_per_seq == 0)
    def _():
        st_ref[...] = jnp.zeros_like(st_ref)

    lbp = lbp_ref[...]
    e = jnp.exp(lbp - jnp.max(lbp, axis=0, keepdims=True))
    p = e / jnp.sum(e, axis=0, keepdims=True)
    lb_all = jnp.sum(p[1:layer + 1], axis=0, keepdims=True) if layer > 0 else jnp.zeros_like(p[0:1])

    wl = heads * HEAD_DIM
    col_w = min(MXU_WIDTH, wl)
    pieces = [(r, cb) for r in range(N_PROJ) for cb in range(wl // col_w)]
    assert len(pieces) <= 3 * heads
    fill_hooks = list(range(1, len(pieces) + 1))

    def run(cur_ref, nxt_ref):
        def chunk(s, carry):
            def project_slice(hook_idx):
                if hook_idx not in fill_hooks:
                    return
                r, cb = pieces[fill_hooks.index(hook_idx)]
                rws = slice(r * REC_CHUNK, (r + 1) * REC_CHUNK)
                cols = slice(cb * col_w, (cb + 1) * col_w)
                nxt_ref[s, rws, cols] = _dot(hn_ref[rws, :], w_ref[s, :, cols])
            rows = pl.ds(pl.multiple_of(s * REC_CHUNK, REC_CHUNK), REC_CHUNK)
            _rec_chunk(cur_ref, rows, lb_all, gain_ref, sums_ref, lvl_ref, o_ref, st_ref, heads,
                       after_head=project_slice)
            return carry
        lax.fori_loop(0, N_PROJ, chunk, 0)

    @pl.when(n % 2 == 0)
    def _():
        run(pa_ref, pb_ref)

    @pl.when(n % 2 == 1)
    def _():
        run(pb_ref, pa_ref)


def _hgrn_mixer(h, w_in, hgrn_lb, out_norm_gain_l, *, layer_idx, batch, seq, layer):
    m, d = h.shape
    depth = hgrn_lb.shape[0]
    groups, wl = w_in.shape[1], w_in.shape[4]
    hb = wl // HEAD_DIM
    rb = N_PROJ * REC_CHUNK
    assert seq % rb == 0
    steps_per_seq = seq // rb
    steps_per_group = batch * steps_per_seq
    n_steps = groups * steps_per_group
    sums_np, lvl_np = _rec_constants()
    sums = jnp.asarray(sums_np, BF16)
    lvl = jnp.asarray(lvl_np)
    nxt = lambda n: jnp.minimum(n + 1, n_steps - 1)

    return pl.pallas_call(
        functools.partial(_hgrn_kernel, layer=layer, heads=hb, steps_per_seq=steps_per_seq),
        grid=(n_steps,),
        in_specs=[
            pl.BlockSpec((rb, d), lambda n: (0, 0)),
            pl.BlockSpec((rb, d), lambda n: (nxt(n) % steps_per_group, 0)),
            pl.BlockSpec((None, None, N_PROJ, d, wl), lambda n: (layer_idx, nxt(n) // steps_per_group, 0, 0, 0),
                         pipeline_mode=pl.Buffered(1)),
            pl.BlockSpec((depth, wl), lambda n: (0, n // steps_per_group)),
            pl.BlockSpec((1, wl), lambda n: (0, n // steps_per_group)),
            pl.BlockSpec(sums.shape, lambda n: (0, 0)),
            pl.BlockSpec(lvl.shape, lambda n: (0, 0)),
        ],
        out_specs=pl.BlockSpec((rb, wl), lambda n: (n % steps_per_group, n // steps_per_group)),
        out_shape=jax.ShapeDtypeStruct((m, d), BF16),
        scratch_shapes=[pltpu.VMEM((N_PROJ, rb, wl), F32), pltpu.VMEM((N_PROJ, rb, wl), F32),
                        pltpu.VMEM((hb, HEAD_DIM, HEAD_DIM), F32)],
        compiler_params=pltpu.CompilerParams(
            dimension_semantics=("arbitrary",),
            vmem_limit_bytes=V7X_VMEM_LIMIT),
        name="hgrn_proj_recurrence",
    )(h, h, w_in, hgrn_lb, out_norm_gain_l.reshape(1, d), sums, lvl)


def kernel(x, c, norm_mix_gain, norm_ffn_gain, w_ada, b_ada, pool_w, pool_scale, hgrn_w_in, hgrn_w_out,
           hgrn_norm_gain, hgrn_lb, w_ffn_in, w_ffn_out, final_gain):
    batch, seq, d = x.shape
    depth = w_ada.shape[0]
    mods = _modulation(c, w_ada, b_ada).reshape(depth, batch, N_MOD, d)
    x2 = x.reshape(batch * seq, d)
    w_in16, w_out16 = w_ffn_in.astype(BF16), w_ffn_out.astype(BF16)
    pool_w16, hgrn_out16 = pool_w.astype(BF16), hgrn_w_out.astype(BF16)
    n_heads = d // HEAD_DIM
    hb = min(REC_HEADS, n_heads)
    hgrn_in16 = hgrn_w_in.reshape(-1, d, N_PROJ, n_heads // hb, hb * HEAD_DIM)
    hgrn_in16 = hgrn_in16.transpose(0, 3, 2, 1, 4).astype(BF16)
    assert depth % 2 == 0, "layers alternate pooling / HGRN2; each pooling layer feeds the HGRN2 layer after it"
    h = None
    for layer in range(depth):
        j = layer // 2
        fin = final_gain if layer == depth - 1 else None
        if layer % 2 == 0:
            mixer_args = (norm_mix_gain[layer], pool_w16, pool_scale[j], j)
            x2, h = _ffn_layer(x2, mods[layer], norm_ffn_gain[layer], w_in16, w_out16, layer=layer, seq=seq,
                               mixer="pool", mixer_args=mixer_args, final_gain=fin,
                               next_mixer=(mods[layer + 1], norm_mix_gain[layer + 1]))
        else:
            og = _hgrn_mixer(h, hgrn_in16, hgrn_lb, hgrn_norm_gain[j], layer_idx=j, batch=batch, seq=seq,
                             layer=layer)
            mixer_args = (og, hgrn_out16, j)
            x2 = _ffn_layer(x2, mods[layer], norm_ffn_gain[layer], w_in16, w_out16, layer=layer, seq=seq,
                            mixer="hgrn", mixer_args=mixer_args, final_gain=fin)
    return x2.reshape(batch, seq, d)
```

```python
import functools

import numpy as np
import jax
import jax.numpy as jnp
from jax import lax
from jax.experimental import pallas as pl
from jax.experimental.pallas import tpu as pltpu

EPS = 1e-6
LOG2_E = 1.4426950408889634
POOL_WINDOWS = (2, 4, 8, 16)
POOL_HALO = 16
HEAD_DIM = 128
N_MOD = 6
FFN_SUBTILES = 2
N_PROJ = 4
REC_HEADS = 8
REC_CHUNK = 128
REC_LEVELS = 7
REC_FINE_LEVELS = (4, 5)
MXU_WIDTH = 256
V7X_VMEM_LIMIT = 56 * 1024 * 1024

F32 = jnp.float32
BF16 = jnp.bfloat16


def _sigmoid(x):
    return 1.0 / (1.0 + jnp.exp(-x))


def _mod_norm(x, gain, shift, scale):
    ms = jnp.mean(x * x, axis=-1, keepdims=True)
    return x * lax.rsqrt(ms + EPS) * (gain * (1.0 + scale)) + shift


def _dot(a, b):
    return jnp.dot(a, b, preferred_element_type=F32)


def _dot_nt(a, b):
    return lax.dot_general(a, b, (((1,), (1,)), ((), ())), preferred_element_type=F32)


def _dot_tn(a, b):
    return lax.dot_general(a, b, (((0,), (0,)), ((), ())), preferred_element_type=F32)


def _mod_kernel(c_ref, w_ref, b_ref, o_ref, *, batch):
    cpad = c_ref[...]
    cond = (cpad * _sigmoid(cpad)).astype(BF16)
    res = _dot(cond, w_ref[...].astype(BF16)) + b_ref[...]
    o_ref[...] = res[:batch]


def _modulation(c, w_ada, b_ada):
    depth, d, n = w_ada.shape
    batch = c.shape[0]
    rows = 16
    cpad = jnp.zeros((rows, d), F32).at[:batch].set(c)
    tn = min(n, 1024)
    return pl.pallas_call(
        functools.partial(_mod_kernel, batch=batch),
        grid=(depth, n // tn),
        in_specs=[
            pl.BlockSpec((rows, d), lambda l, j: (0, 0)),
            pl.BlockSpec((None, d, tn), lambda l, j: (l, 0, j)),
            pl.BlockSpec((None, 1, tn), lambda l, j: (l, 0, j)),
        ],
        out_specs=pl.BlockSpec((None, batch, tn), lambda l, j: (l, 0, j)),
        out_shape=jax.ShapeDtypeStruct((depth, batch, n), F32),
        compiler_params=pltpu.CompilerParams(
            dimension_semantics=("parallel", "parallel"),
            vmem_limit_bytes=V7X_VMEM_LIMIT),
        name="adaln_mod",
    )(cpad, w_ada, b_ada.reshape(depth, 1, n))


def _pool_mixer_rows(x, r0, xh_ref, gain, shift, scale, pw_ref, ps_ref, hs_ref, first_tile, pos_base):
    n, d = x.shape
    g_dim = d // len(POOL_WINDOWS)
    if r0 == 0:
        hh = _mod_norm(xh_ref[...], gain, shift, scale)
        hs_ref[0:POOL_HALO, :] = jnp.where(first_tile, 0.0, hh)
    hs_ref[POOL_HALO + r0:POOL_HALO + r0 + n, :] = _mod_norm(x, gain, shift, scale)
    row = lax.broadcasted_iota(jnp.int32, (POOL_HALO, g_dim), 0)
    pos = (pos_base + row + 1).astype(F32)
    ys = []
    for g, w in enumerate(POOL_WINDOWS):
        lanes = slice(g * g_dim, (g + 1) * g_dim)
        s = hs_ref[r0:POOL_HALO + r0 + n, lanes]
        span = 1
        while span < w:
            s = s + pltpu.roll(s, span, axis=0)
            span *= 2
        s = s[POOL_HALO:]
        hg = hs_ref[POOL_HALO + r0:POOL_HALO + r0 + n, lanes]
        if r0 == 0:
            top = s[:POOL_HALO] / jnp.minimum(pos, float(w)) - hg[:POOL_HALO]
            rest = s[POOL_HALO:] * (1.0 / w) - hg[POOL_HALO:]
            dg = jnp.concatenate([top, rest], axis=0)
        else:
            dg = s * (1.0 / w) - hg
        ys.append(_dot(dg.astype(BF16), pw_ref[g]))
    return jnp.concatenate(ys, axis=1) * ps_ref[...]


def _cast_kernel(a_ref, b_ref, a16_ref, b16_ref):
    a16_ref[...] = a_ref[...].astype(BF16)
    b16_ref[...] = b_ref[...].astype(BF16)


def _cast_ffn_weights(w_in, w_out, layer, steps=16):
    _, d, n2 = w_in.shape
    _, d_ff, _ = w_out.shape
    ra, rb = d // steps, d_ff // steps
    assert ra * steps == d and rb * steps == d_ff and ra % 16 == 0 and rb % 16 == 0
    return pl.pallas_call(
        _cast_kernel,
        grid=(steps,),
        in_specs=[pl.BlockSpec((None, ra, n2), lambda s: (layer, s, 0)),
                  pl.BlockSpec((None, rb, d), lambda s: (layer, s, 0))],
        out_specs=[pl.BlockSpec((ra, n2), lambda s: (s, 0)), pl.BlockSpec((rb, d), lambda s: (s, 0))],
        out_shape=[jax.ShapeDtypeStruct((d, n2), BF16), jax.ShapeDtypeStruct((d_ff, d), BF16)],
        compiler_params=pltpu.CompilerParams(
            dimension_semantics=("parallel",), vmem_limit_bytes=V7X_VMEM_LIMIT),
        name="cast_ffn_weights",
    )(w_in, w_out)


def _ffn_kernel(*refs, mixer, final, feeds_next, casts_next, nf, tiles_per_seq, tm):
    if mixer == "pool":
        (x_ref, xh_ref, mod_ref, gm_ref, gf_ref, pw_ref, ps_ref, wa_ref, wb_ref, wo_ref) = refs[:10]
        rest = refs[10:]
    else:
        (x_ref, og_ref, mod_ref, gf_ref, wout_ref, wa_ref, wb_ref, wo_ref) = refs[:8]
        rest = refs[8:]
    if final:
        fg_ref, rest = rest[0], rest[1:]
    if feeds_next:
        modn_ref, gn_ref, rest = rest[0], rest[1], rest[2:]
    if casts_next:
        wi32_ref, wo32_ref, rest = rest[0], rest[1], rest[2:]
    o_ref, rest = rest[0], rest[1:]
    if feeds_next:
        hn_ref, rest = rest[0], rest[1:]
    if casts_next:
        wi16_ref, wo16_ref, rest = rest[0], rest[1], rest[2:]
    if mixer == "pool":
        h2_ref, acc_ref, hs_ref = rest
    else:
        h2_ref, acc_ref = rest

    i = pl.program_id(0)
    f = pl.program_id(1)
    tile_in_seq = i % tiles_per_seq

    def cast_slabs():
        if casts_next:
            wi16_ref[...] = wi32_ref[...].astype(BF16)
            wo16_ref[...] = wo32_ref[...].astype(BF16)

    def ffn_chunk(h2):
        n = h2.shape[0]
        halves = [h2] if n < 2 * MXU_WIDTH else [h2[:n // 2], h2[n // 2:]]
        us = []
        for hh in halves:
            a = _dot(hh, wa_ref[...])
            b = _dot(hh, wb_ref[...])
            us.append((a * _sigmoid(a) * b).astype(BF16))
        return jnp.concatenate([_dot(u, wo_ref[...]) for u in us], axis=0)

    @pl.when(f == 0)
    def _():
        cast_slabs()
        sub = tm // FFN_SUBTILES
        if mixer == "hgrn":
            ys = [_dot(og_ref[r0:r0 + sub, :], wout_ref[...]) for r0 in range(0, tm, sub)]
        for t, r0 in enumerate(range(0, tm, sub)):
            rows = slice(r0, r0 + sub)
            x = x_ref[rows, :]
            if mixer == "pool":
                y = _pool_mixer_rows(x, r0, xh_ref, gm_ref[...], mod_ref[0:1, :], mod_ref[1:2, :],
                                     pw_ref, ps_ref, hs_ref, tile_in_seq == 0, tile_in_seq * tm)
            else:
                y = ys[t]
            xm = x + mod_ref[2:3, :] * y
            o_ref[rows, :] = xm
            h2 = _mod_norm(xm, gf_ref[...], mod_ref[3:4, :], mod_ref[4:5, :]).astype(BF16)
            h2_ref[rows, :] = h2
            acc_ref[rows, :] = ffn_chunk(h2)

    @pl.when(jnp.logical_and(f > 0, f < nf - 1))
    def _():
        cast_slabs()
        acc_ref[...] += ffn_chunk(h2_ref[...])

    @pl.when(f == nf - 1)
    def _():
        cast_slabs()
        sub = tm // FFN_SUBTILES
        for r0 in range(0, tm, sub):
            rows = slice(r0, r0 + sub)
            acc = acc_ref[rows, :] + ffn_chunk(h2_ref[rows, :])
            out = o_ref[rows, :] + mod_ref[5:6, :] * acc
            if final:
                ms = jnp.mean(out * out, axis=-1, keepdims=True)
                out = out * lax.rsqrt(ms + EPS) * fg_ref[...]
            o_ref[rows, :] = out
            if feeds_next:
                hn = _mod_norm(out, gn_ref[...], modn_ref[0:1, :], modn_ref[1:2, :])
                hn_ref[rows, :] = hn.astype(BF16)


def _ffn_layer(x2, mod_l, norm_ffn_gain_l, w_in, w_out, *, seq, mixer, mixer_args, final_gain=None,
               next_mixer=None, cast_next=None, tm=512, tf=512):
    m, d = x2.shape
    d_ff = w_out.shape[0]
    tm = min(tm, seq)
    tf = min(tf, d_ff)
    assert seq % tm == 0 and d_ff % tf == 0 and tm % (FFN_SUBTILES * POOL_HALO) == 0
    nf = d_ff // tf
    assert nf >= 2, "the first and the last d_ff step are distinct code paths"
    tiles_per_seq = seq // tm
    row = lambda v: v.reshape(1, d)
    x_spec = pl.BlockSpec((tm, d), lambda i, f: (i, 0))
    mod_spec = pl.BlockSpec((None, N_MOD, d), lambda i, f: (i // tiles_per_seq, 0, 0))
    vec_spec = pl.BlockSpec((1, d), lambda i, f: (0, 0))
    ffn_specs = [
        pl.BlockSpec((d, tf), lambda i, f: (0, f)),
        pl.BlockSpec((d, tf), lambda i, f: (0, nf + f)),
        pl.BlockSpec((tf, d), lambda i, f: (f, 0)),
    ]
    scratch = [pltpu.VMEM((tm, d), BF16), pltpu.VMEM((tm, d), F32)]
    if mixer == "pool":
        norm_mix_gain_l, pool_w, pool_scale, j = mixer_args
        _, n_groups, g_dim, _ = pool_w.shape
        halo_blocks = tm // POOL_HALO
        args = [x2, x2, mod_l, row(norm_mix_gain_l), row(norm_ffn_gain_l), pool_w, row(pool_scale)]
        specs = [x_spec,
                 pl.BlockSpec((POOL_HALO, d), lambda i, f: (jnp.maximum(i * halo_blocks - 1, 0), 0)),
                 mod_spec, vec_spec, vec_spec,
                 pl.BlockSpec((None, n_groups, g_dim, g_dim), lambda i, f: (j, 0, 0, 0)),
                 vec_spec]
        scratch.append(pltpu.VMEM((tm + POOL_HALO, d), F32))
    else:
        og, w_mix_out, j = mixer_args
        args = [x2, og, mod_l, row(norm_ffn_gain_l), w_mix_out]
        specs = [x_spec, pl.BlockSpec((tm, d), lambda i, f: (i, 0)), mod_spec, vec_spec,
                 pl.BlockSpec((None, d, d), lambda i, f: (j, 0, 0), pipeline_mode=pl.Buffered(1))]
    args += [w_in, w_in, w_out]
    specs += ffn_specs
    if final_gain is not None:
        args.append(row(final_gain))
        specs.append(vec_spec)
    out_specs = [pl.BlockSpec((tm, d), lambda i, f: (i, 0))]
    out_shape = [jax.ShapeDtypeStruct((m, d), F32)]
    if next_mixer is not None:
        args += [next_mixer[0], row(next_mixer[1])]
        specs += [mod_spec, vec_spec]
        out_specs.append(pl.BlockSpec((tm, d), lambda i, f: (i, 0)))
        out_shape.append(jax.ShapeDtypeStruct((m, d), BF16))
    if cast_next is not None:
        w_in32, w_out32, nxt = cast_next
        n_i = m // tm
        ri, ci, ro = d // n_i, 2 * d_ff // nf, d_ff // (n_i * nf)
        assert ri * n_i == d and ci * nf == 2 * d_ff and ro * n_i * nf == d_ff
        assert ri % 16 == 0 and ro % 16 == 0 and ci % 128 == 0
        args += [w_in32, w_out32]
        specs += [pl.BlockSpec((None, ri, ci), lambda i, f: (nxt, i, f)),
                  pl.BlockSpec((None, ro, d), lambda i, f: (nxt, i * nf + f, 0))]
        out_specs += [pl.BlockSpec((ri, ci), lambda i, f: (i, f)),
                      pl.BlockSpec((ro, d), lambda i, f: (i * nf + f, 0))]
        out_shape += [jax.ShapeDtypeStruct((d, 2 * d_ff), BF16), jax.ShapeDtypeStruct((d_ff, d), BF16)]
    return pl.pallas_call(
        functools.partial(_ffn_kernel, mixer=mixer, final=final_gain is not None,
                          feeds_next=next_mixer is not None, casts_next=cast_next is not None,
                          nf=nf, tiles_per_seq=tiles_per_seq, tm=tm),
        grid=(m // tm, nf),
        in_specs=specs,
        out_specs=out_specs,
        out_shape=out_shape,
        scratch_shapes=scratch,
        compiler_params=pltpu.CompilerParams(
            dimension_semantics=("parallel", "arbitrary"),
            vmem_limit_bytes=V7X_VMEM_LIMIT),
        name="mixer_out_ffn_" + mixer,
    )(*args)


def _level_split(l):
    n = REC_CHUNK >> l
    return n, n // 2


def _rec_constants():
    c = REC_CHUNK
    t = np.arange(c)[:, None]
    u = np.arange(c)[None, :]
    mats = [u <= t]
    level = np.full((c, c), -1, np.int32)
    for l in range(REC_LEVELS):
        n, half = _level_split(l)
        mid = (t // n) * n + half
        if l in REC_FINE_LEVELS:
            mats.append(np.where(t >= mid, (u >= mid) & (u <= t), (u > t) & (u < mid)))
        same = (t // n) == (u // n)
        level[same & (t % n >= half) & (u % n < half)] = l
    level[np.arange(c), np.arange(c)] = REC_LEVELS
    return np.concatenate(mats, axis=0).astype(np.float32), level


def _coarse_level_operand(q, k, cum, l):
    n, half = _level_split(l)
    pieces = []
    for lo in range(0, REC_CHUNK, n):
        mid, hi = lo + half, lo + n
        ref = cum[mid - 1:mid]
        pieces.append(k[lo:mid] * jnp.exp2(ref - cum[lo:mid]))
        pieces.append(q[mid:hi] * jnp.exp2(cum[mid:hi] - ref))
    return jnp.concatenate(pieces, axis=0)


def _rec_chunk(proj_ref, rows, lb_all, gain_ref, sums_ref, lvl_ref, o_ref, st_ref, heads, after_head=None):
    c = REC_CHUNK
    lvl = lvl_ref[...]
    in_level = [lvl == l for l in range(REC_LEVELS + 1)]
    rowi = lax.broadcasted_iota(jnp.int32, (c, HEAD_DIM), 0)
    second_half = {l: (rowi & _level_split(l)[1]) != 0 for l in REC_FINE_LEVELS + (REC_LEVELS - 1,)}
    sums = sums_ref[...]
    hook = iter(range(3 * heads + 1))

    def fill():
        if after_head is not None:
            after_head(next(hook))

    fill()

    gates, parts = [], []
    for j in range(heads):
        lanes = slice(j * HEAD_DIM, (j + 1) * HEAD_DIM)
        lb = lb_all[:, lanes]
        fgt = lb + (1.0 - lb) * _sigmoid(proj_ref[1, rows, lanes])
        logf = jnp.log(fgt) * LOG2_E
        hi = logf.astype(BF16)
        lo = (logf - hi.astype(F32)).astype(BF16)
        gates.append(fgt)
        parts.append(_dot(sums, jnp.concatenate([hi, lo], axis=1)))
        fill()

    all_scores, cums = [], []
    for j in range(heads):
        lanes = slice(j * HEAD_DIM, (j + 1) * HEAD_DIM)
        q = proj_ref[0, rows, lanes]
        fgt = gates[j]
        k = 1.0 - fgt
        ex = parts[j][:, :HEAD_DIM] + parts[j][:, HEAD_DIM:]
        cum = ex[:c]
        scores = jnp.zeros((c, c), F32)
        for l in range(REC_LEVELS):
            if l in REC_FINE_LEVELS:
                i = 1 + REC_FINE_LEVELS.index(l)
                xl = jnp.where(second_half[l], q, k) * jnp.exp2(ex[i * c:(i + 1) * c])
            elif l == REC_LEVELS - 1:
                xl = jnp.where(second_half[l], q * fgt, k)
            else:
                xl = _coarse_level_operand(q, k, cum, l)
            xl = xl.astype(BF16)
            scores = jnp.where(in_level[l], _dot_nt(xl, xl), scores)
        scores = jnp.where(in_level[REC_LEVELS], jnp.sum(q * k, axis=-1, keepdims=True), scores)
        all_scores.append(scores.astype(BF16))
        cums.append(cum)
        fill()

    for j in range(heads):
        lanes = slice(j * HEAD_DIM, (j + 1) * HEAD_DIM)
        q = proj_ref[0, rows, lanes]
        v = proj_ref[2, rows, lanes].astype(BF16)
        gate = proj_ref[3, rows, lanes]
        k = 1.0 - gates[j]
        cum = cums[j]
        st = st_ref[j]
        end = cum[c - 1:c]
        qd = (q * jnp.exp2(cum)).astype(BF16)
        o = _dot(all_scores[j], v) + _dot_nt(qd, st.astype(BF16))
        kd = (k * jnp.exp2(end - cum)).astype(BF16)
        st_ref[j] = st * jnp.exp2(end) + _dot_tn(v, kd)

        ms = jnp.mean(o * o, axis=-1, keepdims=True)
        on = o * lax.rsqrt(ms + EPS) * gain_ref[:, lanes] * (gate * _sigmoid(gate))
        o_ref[rows, lanes] = on.astype(BF16)
        fill()


def _hgrn_kernel(h0_ref, hn_ref, w_ref, lbp_ref, gain_ref, sums_ref, lvl_ref, o_ref,
                 pa_ref, pb_ref, st_ref, *, layer, heads, steps_per_seq):
    n = pl.program_id(0)

    @pl.when(n == 0)
    def _():
        for part in range(N_PROJ):
            pa_ref[part] = _dot(h0_ref[...], w_ref[part])

    @pl.when(n % steps_per_seq == 0)
    def _():
        st_ref[...] = jnp.zeros_like(st_ref)

    lbp = lbp_ref[...]
    e = jnp.exp(lbp - jnp.max(lbp, axis=0, keepdims=True))
    p = e / jnp.sum(e, axis=0, keepdims=True)
    lb_all = jnp.sum(p[1:layer + 1], axis=0, keepdims=True) if layer > 0 else jnp.zeros_like(p[0:1])

    wl = heads * HEAD_DIM
    col_w = min(MXU_WIDTH, wl)
    pieces = [(r, cb) for r in range(N_PROJ) for cb in range(wl // col_w)]
    assert len(pieces) <= 3 * heads
    fill_hooks = list(range(1, len(pieces) + 1))

    def run(cur_ref, nxt_ref):
        def chunk(s, carry):
            def project_slice(hook_idx):
                if hook_idx not in fill_hooks:
                    return
                r, cb = pieces[fill_hooks.index(hook_idx)]
                rws = slice(r * REC_CHUNK, (r + 1) * REC_CHUNK)
                cols = slice(cb * col_w, (cb + 1) * col_w)
                nxt_ref[s, rws, cols] = _dot(hn_ref[rws, :], w_ref[s, :, cols])
            rows = pl.ds(pl.multiple_of(s * REC_CHUNK, REC_CHUNK), REC_CHUNK)
            _rec_chunk(cur_ref, rows, lb_all, gain_ref, sums_ref, lvl_ref, o_ref, st_ref, heads,
                       after_head=project_slice)
            return carry
        lax.fori_loop(0, N_PROJ, chunk, 0)

    @pl.when(n % 2 == 0)
    def _():
        run(pa_ref, pb_ref)

    @pl.when(n % 2 == 1)
    def _():
        run(pb_ref, pa_ref)


def _hgrn_mixer(h, w_in, hgrn_lb, out_norm_gain_l, *, layer_idx, batch, seq, layer):
    m, d = h.shape
    depth = hgrn_lb.shape[0]
    groups, wl = w_in.shape[1], w_in.shape[4]
    hb = wl // HEAD_DIM
    rb = N_PROJ * REC_CHUNK
    assert seq % rb == 0
    steps_per_seq = seq // rb
    steps_per_group = batch * steps_per_seq
    n_steps = groups * steps_per_group
    sums_np, lvl_np = _rec_constants()
    sums = jnp.asarray(sums_np, BF16)
    lvl = jnp.asarray(lvl_np)
    nxt = lambda n: jnp.minimum(n + 1, n_steps - 1)

    return pl.pallas_call(
        functools.partial(_hgrn_kernel, layer=layer, heads=hb, steps_per_seq=steps_per_seq),
        grid=(n_steps,),
        in_specs=[
            pl.BlockSpec((rb, d), lambda n: (0, 0)),
            pl.BlockSpec((rb, d), lambda n: (nxt(n) % steps_per_group, 0)),
            pl.BlockSpec((None, None, N_PROJ, d, wl), lambda n: (layer_idx, nxt(n) // steps_per_group, 0, 0, 0),
                         pipeline_mode=pl.Buffered(1)),
            pl.BlockSpec((depth, wl), lambda n: (0, n // steps_per_group)),
            pl.BlockSpec((1, wl), lambda n: (0, n // steps_per_group)),
            pl.BlockSpec(sums.shape, lambda n: (0, 0)),
            pl.BlockSpec(lvl.shape, lambda n: (0, 0)),
        ],
        out_specs=pl.BlockSpec((rb, wl), lambda n: (n % steps_per_group, n // steps_per_group)),
        out_shape=jax.ShapeDtypeStruct((m, d), BF16),
        scratch_shapes=[pltpu.VMEM((N_PROJ, rb, wl), F32), pltpu.VMEM((N_PROJ, rb, wl), F32),
                        pltpu.VMEM((hb, HEAD_DIM, HEAD_DIM), F32)],
        compiler_params=pltpu.CompilerParams(
            dimension_semantics=("arbitrary",),
            vmem_limit_bytes=V7X_VMEM_LIMIT),
        name="hgrn_proj_recurrence",
    )(h, h, w_in, hgrn_lb, out_norm_gain_l.reshape(1, d), sums, lvl)


def kernel(x, c, norm_mix_gain, norm_ffn_gain, w_ada, b_ada, pool_w, pool_scale, hgrn_w_in, hgrn_w_out,
           hgrn_norm_gain, hgrn_lb, w_ffn_in, w_ffn_out, final_gain):
    batch, seq, d = x.shape
    depth = w_ada.shape[0]
    mods = _modulation(c, w_ada, b_ada).reshape(depth, batch, N_MOD, d)
    x2 = x.reshape(batch * seq, d)
    w_in16, w_out16 = _cast_ffn_weights(w_ffn_in, w_ffn_out, 0)
    pool_w16, hgrn_out16 = pool_w.astype(BF16), hgrn_w_out.astype(BF16)
    n_heads = d // HEAD_DIM
    hb = min(REC_HEADS, n_heads)
    hgrn_in16 = hgrn_w_in.reshape(-1, d, N_PROJ, n_heads // hb, hb * HEAD_DIM)
    hgrn_in16 = hgrn_in16.transpose(0, 3, 2, 1, 4).astype(BF16)
    assert depth % 2 == 0, "layers alternate pooling / HGRN2; each pooling layer feeds the HGRN2 layer after it"
    h = None
    for layer in range(depth):
        j = layer // 2
        last = layer == depth - 1
        fin = final_gain if last else None
        cast_next = None if last else (w_ffn_in, w_ffn_out, layer + 1)
        if layer % 2 == 0:
            mixer_args = (norm_mix_gain[layer], pool_w16, pool_scale[j], j)
            x2, h, w_in16, w_out16 = _ffn_layer(
                x2, mods[layer], norm_ffn_gain[layer], w_in16, w_out16, seq=seq, mixer="pool",
                mixer_args=mixer_args, final_gain=fin, cast_next=cast_next,
                next_mixer=(mods[layer + 1], norm_mix_gain[layer + 1]))
        else:
            og = _hgrn_mixer(h, hgrn_in16, hgrn_lb, hgrn_norm_gain[j], layer_idx=j, batch=batch, seq=seq,
                             layer=layer)
            mixer_args = (og, hgrn_out16, j)
            outs = _ffn_layer(x2, mods[layer], norm_ffn_gain[layer], w_in16, w_out16, seq=seq, mixer="hgrn",
                              mixer_args=mixer_args, final_gain=fin, cast_next=cast_next)
            x2 = outs[0]
            if cast_next is not None:
                w_in16, w_out16 = outs[1:]
    return x2.reshape(batch, seq, d)
```

```python
import functools

import numpy as np
import jax
import jax.numpy as jnp
from jax import lax
from jax.experimental import pallas as pl
from jax.experimental.pallas import tpu as pltpu

EPS = 1e-6
LOG2_E = 1.4426950408889634
POOL_WINDOWS = (2, 4, 8, 16)
POOL_HALO = 16
HEAD_DIM = 128
N_MOD = 6
FFN_SUBTILES = 2
N_PROJ = 4
REC_HEADS = 8
REC_CHUNK = 128
REC_LEVELS = 7
REC_FINE_LEVELS = (4, 5)
MXU_WIDTH = 256
V7X_VMEM_LIMIT = 56 * 1024 * 1024

F32 = jnp.float32
BF16 = jnp.bfloat16


def _sigmoid(x):
    return 1.0 / (1.0 + jnp.exp(-x))


def _mod_norm(x, gain, shift, scale):
    ms = jnp.mean(x * x, axis=-1, keepdims=True)
    return x * lax.rsqrt(ms + EPS) * (gain * (1.0 + scale)) + shift


def _dot(a, b):
    return jnp.dot(a, b, preferred_element_type=F32)


def _dot_nt(a, b):
    return lax.dot_general(a, b, (((1,), (1,)), ((), ())), preferred_element_type=F32)


def _dot_tn(a, b):
    return lax.dot_general(a, b, (((0,), (0,)), ((), ())), preferred_element_type=F32)


def _mod_kernel(c_ref, w_ref, b_ref, o_ref, *, batch):
    cpad = c_ref[...]
    cond = (cpad * _sigmoid(cpad)).astype(BF16)
    res = _dot(cond, w_ref[...].astype(BF16)) + b_ref[...]
    o_ref[...] = res[:batch]


def _modulation(c, w_ada, b_ada):
    depth, d, n = w_ada.shape
    batch = c.shape[0]
    rows = 16
    cpad = jnp.zeros((rows, d), F32).at[:batch].set(c)
    tn = min(n, 1024)
    return pl.pallas_call(
        functools.partial(_mod_kernel, batch=batch),
        grid=(depth, n // tn),
        in_specs=[
            pl.BlockSpec((rows, d), lambda l, j: (0, 0)),
            pl.BlockSpec((None, d, tn), lambda l, j: (l, 0, j)),
            pl.BlockSpec((None, 1, tn), lambda l, j: (l, 0, j)),
        ],
        out_specs=pl.BlockSpec((None, batch, tn), lambda l, j: (l, 0, j)),
        out_shape=jax.ShapeDtypeStruct((depth, batch, n), F32),
        compiler_params=pltpu.CompilerParams(
            dimension_semantics=("parallel", "parallel"),
            vmem_limit_bytes=V7X_VMEM_LIMIT),
        name="adaln_mod",
    )(cpad, w_ada, b_ada.reshape(depth, 1, n))


def _pool_mixer_rows(x, r0, xh_ref, gain, shift, scale, pw_ref, ps_ref, hs_ref, first_tile, pos_base):
    n, d = x.shape
    g_dim = d // len(POOL_WINDOWS)
    if r0 == 0:
        hh = _mod_norm(xh_ref[...], gain, shift, scale)
        hs_ref[0:POOL_HALO, :] = jnp.where(first_tile, 0.0, hh)
    hs_ref[POOL_HALO + r0:POOL_HALO + r0 + n, :] = _mod_norm(x, gain, shift, scale)
    row = lax.broadcasted_iota(jnp.int32, (POOL_HALO, g_dim), 0)
    pos = (pos_base + row + 1).astype(F32)
    ys = []
    for g, w in enumerate(POOL_WINDOWS):
        lanes = slice(g * g_dim, (g + 1) * g_dim)
        s = hs_ref[r0:POOL_HALO + r0 + n, lanes]
        span = 1
        while span < w:
            s = s + pltpu.roll(s, span, axis=0)
            span *= 2
        s = s[POOL_HALO:]
        hg = hs_ref[POOL_HALO + r0:POOL_HALO + r0 + n, lanes]
        if r0 == 0:
            top = s[:POOL_HALO] / jnp.minimum(pos, float(w)) - hg[:POOL_HALO]
            rest = s[POOL_HALO:] * (1.0 / w) - hg[POOL_HALO:]
            dg = jnp.concatenate([top, rest], axis=0)
        else:
            dg = s * (1.0 / w) - hg
        ys.append(_dot(dg.astype(BF16), pw_ref[g]))
    return jnp.concatenate(ys, axis=1) * ps_ref[...]


def _cast_kernel(a_ref, b_ref, a16_ref, b16_ref):
    a16_ref[...] = a_ref[...].astype(BF16)
    b16_ref[...] = b_ref[...].astype(BF16)


def _cast_ffn_weights(w_in, w_out, layer, steps=16):
    _, d, n2 = w_in.shape
    _, d_ff, _ = w_out.shape
    ra, rb = d // steps, d_ff // steps
    assert ra * steps == d and rb * steps == d_ff and ra % 16 == 0 and rb % 16 == 0
    return pl.pallas_call(
        _cast_kernel,
        grid=(steps,),
        in_specs=[pl.BlockSpec((None, ra, n2), lambda s: (layer, s, 0)),
                  pl.BlockSpec((None, rb, d), lambda s: (layer, s, 0))],
        out_specs=[pl.BlockSpec((ra, n2), lambda s: (s, 0)), pl.BlockSpec((rb, d), lambda s: (s, 0))],
        out_shape=[jax.ShapeDtypeStruct((d, n2), BF16), jax.ShapeDtypeStruct((d_ff, d), BF16)],
        compiler_params=pltpu.CompilerParams(
            dimension_semantics=("parallel",), vmem_limit_bytes=V7X_VMEM_LIMIT),
        name="cast_ffn_weights",
    )(w_in, w_out)


def _ffn_kernel(*refs, mixer, final, feeds_next, casts_next, nf, tiles_per_seq, tm):
    if mixer == "pool":
        (x_ref, xh_ref, mod_ref, gm_ref, gf_ref, pw_ref, ps_ref, wa_ref, wb_ref, wo_ref) = refs[:10]
        rest = refs[10:]
    else:
        (x_ref, og_ref, mod_ref, gf_ref, wout_ref, wa_ref, wb_ref, wo_ref) = refs[:8]
        rest = refs[8:]
    if final:
        fg_ref, rest = rest[0], rest[1:]
    if feeds_next:
        modn_ref, gn_ref, rest = rest[0], rest[1], rest[2:]
    if casts_next:
        wi32_ref, wo32_ref, rest = rest[0], rest[1], rest[2:]
    o_ref, rest = rest[0], rest[1:]
    if feeds_next:
        hn_ref, rest = rest[0], rest[1:]
    if casts_next:
        wi16_ref, wo16_ref, rest = rest[0], rest[1], rest[2:]
    if mixer == "pool":
        h2_ref, acc_ref, hs_ref = rest
    else:
        h2_ref, acc_ref = rest

    i = pl.program_id(0)
    f = pl.program_id(1)
    tile_in_seq = i % tiles_per_seq

    def cast_slabs():
        if casts_next:
            wi16_ref[...] = wi32_ref[...].astype(BF16)
            wo16_ref[...] = wo32_ref[...].astype(BF16)

    def ffn_chunk(h2):
        n = h2.shape[0]
        halves = [h2] if n < 2 * MXU_WIDTH else [h2[:n // 2], h2[n // 2:]]
        us = []
        for hh in halves:
            a = _dot(hh, wa_ref[...])
            b = _dot(hh, wb_ref[...])
            us.append((a * _sigmoid(a) * b).astype(BF16))
        return jnp.concatenate([_dot(u, wo_ref[...]) for u in us], axis=0)

    @pl.when(f == 0)
    def _():
        cast_slabs()
        sub = tm // FFN_SUBTILES
        if mixer == "hgrn":
            ys = [_dot(og_ref[r0:r0 + sub, :], wout_ref[...]) for r0 in range(0, tm, sub)]
        for t, r0 in enumerate(range(0, tm, sub)):
            rows = slice(r0, r0 + sub)
            x = x_ref[rows, :]
            if mixer == "pool":
                y = _pool_mixer_rows(x, r0, xh_ref, gm_ref[...], mod_ref[0:1, :], mod_ref[1:2, :],
                                     pw_ref, ps_ref, hs_ref, tile_in_seq == 0, tile_in_seq * tm)
            else:
                y = ys[t]
            xm = x + mod_ref[2:3, :] * y
            o_ref[rows, :] = xm
            h2 = _mod_norm(xm, gf_ref[...], mod_ref[3:4, :], mod_ref[4:5, :]).astype(BF16)
            h2_ref[rows, :] = h2
            acc_ref[rows, :] = ffn_chunk(h2)

    @pl.when(jnp.logical_and(f > 0, f < nf - 1))
    def _():
        cast_slabs()
        acc_ref[...] += ffn_chunk(h2_ref[...])

    @pl.when(f == nf - 1)
    def _():
        cast_slabs()
        sub = tm // FFN_SUBTILES
        for r0 in range(0, tm, sub):
            rows = slice(r0, r0 + sub)
            acc = acc_ref[rows, :] + ffn_chunk(h2_ref[rows, :])
            out = o_ref[rows, :] + mod_ref[5:6, :] * acc
            if final:
                ms = jnp.mean(out * out, axis=-1, keepdims=True)
                out = out * lax.rsqrt(ms + EPS) * fg_ref[...]
            o_ref[rows, :] = out
            if feeds_next:
                hn = _mod_norm(out, gn_ref[...], modn_ref[0:1, :], modn_ref[1:2, :])
                hn_ref[rows, :] = hn.astype(BF16)


def _ffn_layer(x2, mod_l, norm_ffn_gain_l, w_in, w_out, *, seq, mixer, mixer_args, final_gain=None,
               next_mixer=None, cast_next=None, tm=512, tf=512):
    m, d = x2.shape
    d_ff = w_out.shape[0]
    tm = min(tm, seq)
    tf = min(tf, d_ff)
    assert seq % tm == 0 and d_ff % tf == 0 and tm % (FFN_SUBTILES * POOL_HALO) == 0
    nf = d_ff // tf
    assert nf >= 2, "the first and the last d_ff step are distinct code paths"
    tiles_per_seq = seq // tm
    row = lambda v: v.reshape(1, d)
    x_spec = pl.BlockSpec((tm, d), lambda i, f: (i, 0))
    mod_spec = pl.BlockSpec((None, N_MOD, d), lambda i, f: (i // tiles_per_seq, 0, 0))
    vec_spec = pl.BlockSpec((1, d), lambda i, f: (0, 0))
    ffn_specs = [
        pl.BlockSpec((d, tf), lambda i, f: (0, f)),
        pl.BlockSpec((d, tf), lambda i, f: (0, nf + f)),
        pl.BlockSpec((tf, d), lambda i, f: (f, 0)),
    ]
    scratch = [pltpu.VMEM((tm, d), BF16), pltpu.VMEM((tm, d), F32)]
    if mixer == "pool":
        norm_mix_gain_l, pool_w, pool_scale, j = mixer_args
        _, n_groups, g_dim, _ = pool_w.shape
        halo_blocks = tm // POOL_HALO
        args = [x2, x2, mod_l, row(norm_mix_gain_l), row(norm_ffn_gain_l), pool_w, row(pool_scale)]
        specs = [x_spec,
                 pl.BlockSpec((POOL_HALO, d), lambda i, f: (jnp.maximum(i * halo_blocks - 1, 0), 0)),
                 mod_spec, vec_spec, vec_spec,
                 pl.BlockSpec((None, n_groups, g_dim, g_dim), lambda i, f: (j, 0, 0, 0)),
                 vec_spec]
        scratch.append(pltpu.VMEM((tm + POOL_HALO, d), F32))
    else:
        og, w_mix_out, j = mixer_args
        args = [x2, og, mod_l, row(norm_ffn_gain_l), w_mix_out]
        specs = [x_spec, pl.BlockSpec((tm, d), lambda i, f: (i, 0)), mod_spec, vec_spec,
                 pl.BlockSpec((None, d, d), lambda i, f: (j, 0, 0), pipeline_mode=pl.Buffered(1))]
    args += [w_in, w_in, w_out]
    specs += ffn_specs
    if final_gain is not None:
        args.append(row(final_gain))
        specs.append(vec_spec)
    out_specs = [pl.BlockSpec((tm, d), lambda i, f: (i, 0))]
    out_shape = [jax.ShapeDtypeStruct((m, d), F32)]
    if next_mixer is not None:
        args += [next_mixer[0], row(next_mixer[1])]
        specs += [mod_spec, vec_spec]
        out_specs.append(pl.BlockSpec((tm, d), lambda i, f: (i, 0)))
        out_shape.append(jax.ShapeDtypeStruct((m, d), BF16))
    if cast_next is not None:
        w_in32, w_out32, nxt = cast_next
        n_i = m // tm
        ri, ci, ro = d // n_i, 2 * d_ff // nf, d_ff // (n_i * nf)
        assert ri * n_i == d and ci * nf == 2 * d_ff and ro * n_i * nf == d_ff
        assert ri % 16 == 0 and ro % 16 == 0 and ci % 128 == 0
        args += [w_in32, w_out32]
        specs += [pl.BlockSpec((None, ri, ci), lambda i, f: (nxt, i, f)),
                  pl.BlockSpec((None, ro, d), lambda i, f: (nxt, i * nf + f, 0))]
        out_specs += [pl.BlockSpec((ri, ci), lambda i, f: (i, f)),
                      pl.BlockSpec((ro, d), lambda i, f: (i * nf + f, 0))]
        out_shape += [jax.ShapeDtypeStruct((d, 2 * d_ff), BF16), jax.ShapeDtypeStruct((d_ff, d), BF16)]
    return pl.pallas_call(
        functools.partial(_ffn_kernel, mixer=mixer, final=final_gain is not None,
                          feeds_next=next_mixer is not None, casts_next=cast_next is not None,
                          nf=nf, tiles_per_seq=tiles_per_seq, tm=tm),
        grid=(m // tm, nf),
        in_specs=specs,
        out_specs=out_specs,
        out_shape=out_shape,
        scratch_shapes=scratch,
        compiler_params=pltpu.CompilerParams(
            dimension_semantics=("parallel", "arbitrary"),
            vmem_limit_bytes=V7X_VMEM_LIMIT),
        name="mixer_out_ffn_" + mixer,
    )(*args)


def _level_split(l):
    n = REC_CHUNK >> l
    return n, n // 2


def _rec_constants():
    c = REC_CHUNK
    t = np.arange(c)[:, None]
    u = np.arange(c)[None, :]
    mats = [u <= t]
    level = np.full((c, c), -1, np.int32)
    for l in range(REC_LEVELS):
        n, half = _level_split(l)
        mid = (t // n) * n + half
        if l in REC_FINE_LEVELS:
            mats.append(np.where(t >= mid, (u >= mid) & (u <= t), (u > t) & (u < mid)))
        same = (t // n) == (u // n)
        level[same & (t % n >= half) & (u % n < half)] = l
    level[np.arange(c), np.arange(c)] = REC_LEVELS
    return np.concatenate(mats, axis=0).astype(np.float32), level


def _coarse_level_operand(q, k, cum, l):
    n, half = _level_split(l)
    pieces = []
    for lo in range(0, REC_CHUNK, n):
        mid, hi = lo + half, lo + n
        ref = cum[mid - 1:mid]
        pieces.append(k[lo:mid] * jnp.exp2(ref - cum[lo:mid]))
        pieces.append(q[mid:hi] * jnp.exp2(cum[mid:hi] - ref))
    return jnp.concatenate(pieces, axis=0)


def _rec_chunk(proj_ref, rows, lb_all, gain_ref, sums_ref, lvl_ref, o_ref, st_ref, heads, after_head=None):
    c = REC_CHUNK
    lvl = lvl_ref[...]
    in_level = [lvl == l for l in range(REC_LEVELS + 1)]
    rowi = lax.broadcasted_iota(jnp.int32, (c, HEAD_DIM), 0)
    second_half = {l: (rowi & _level_split(l)[1]) != 0 for l in REC_FINE_LEVELS + (REC_LEVELS - 1,)}
    sums = sums_ref[...]
    hook = iter(range(3 * heads + 1))

    def fill():
        if after_head is not None:
            after_head(next(hook))

    fill()

    gates, parts = [], []
    for j in range(heads):
        lanes = slice(j * HEAD_DIM, (j + 1) * HEAD_DIM)
        lb = lb_all[:, lanes]
        fgt = lb + (1.0 - lb) * _sigmoid(proj_ref[1, rows, lanes])
        logf = jnp.log(fgt) * LOG2_E
        hi = logf.astype(BF16)
        lo = (logf - hi.astype(F32)).astype(BF16)
        gates.append(fgt)
        parts.append(_dot(sums, jnp.concatenate([hi, lo], axis=1)))
        fill()

    all_scores, cums = [], []
    for j in range(heads):
        lanes = slice(j * HEAD_DIM, (j + 1) * HEAD_DIM)
        q = proj_ref[0, rows, lanes]
        fgt = gates[j]
        k = 1.0 - fgt
        ex = parts[j][:, :HEAD_DIM] + parts[j][:, HEAD_DIM:]
        cum = ex[:c]
        scores = jnp.zeros((c, c), F32)
        for l in range(REC_LEVELS):
            if l in REC_FINE_LEVELS:
                i = 1 + REC_FINE_LEVELS.index(l)
                xl = jnp.where(second_half[l], q, k) * jnp.exp2(ex[i * c:(i + 1) * c])
            elif l == REC_LEVELS - 1:
                xl = jnp.where(second_half[l], q * fgt, k)
            else:
                xl = _coarse_level_operand(q, k, cum, l)
            xt = xl.T.astype(BF16)
            xl = xl.astype(BF16)
            scores = jnp.where(in_level[l], _dot(xl, xt), scores)
        scores = jnp.where(in_level[REC_LEVELS], jnp.sum(q * k, axis=-1, keepdims=True), scores)
        all_scores.append(scores.astype(BF16))
        cums.append(cum)
        fill()

    for j in range(heads):
        lanes = slice(j * HEAD_DIM, (j + 1) * HEAD_DIM)
        q = proj_ref[0, rows, lanes]
        v = proj_ref[2, rows, lanes].astype(BF16)
        gate = proj_ref[3, rows, lanes]
        k = 1.0 - gates[j]
        cum = cums[j]
        st = st_ref[j]
        end = cum[c - 1:c]
        qd = (q * jnp.exp2(cum)).astype(BF16)
        o = _dot(all_scores[j], v) + _dot(qd, st.T.astype(BF16))
        kd = (k * jnp.exp2(end - cum)).astype(BF16)
        st_ref[j] = st * jnp.exp2(end) + _dot_tn(v, kd)

        ms = jnp.mean(o * o, axis=-1, keepdims=True)
        on = o * lax.rsqrt(ms + EPS) * gain_ref[:, lanes] * (gate * _sigmoid(gate))
        o_ref[rows, lanes] = on.astype(BF16)
        fill()


def _hgrn_kernel(h0_ref, hn_ref, w_ref, lbp_ref, gain_ref, sums_ref, lvl_ref, o_ref,
                 pa_ref, pb_ref, st_ref, *, layer, heads, steps_per_seq):
    n = pl.program_id(0)

    @pl.when(n == 0)
    def _():
        for part in range(N_PROJ):
            pa_ref[part] = _dot(h0_ref[...], w_ref[part])

    @pl.when(n % steps_per_seq == 0)
    def _():
        st_ref[...] = jnp.zeros_like(st_ref)

    lbp = lbp_ref[...]
    e = jnp.exp(lbp - jnp.max(lbp, axis=0, keepdims=True))
    p = e / jnp.sum(e, axis=0, keepdims=True)
    lb_all = jnp.sum(p[1:layer + 1], axis=0, keepdims=True) if layer > 0 else jnp.zeros_like(p[0:1])

    wl = heads * HEAD_DIM
    col_w = min(MXU_WIDTH, wl)
    pieces = [(r, cb) for r in range(N_PROJ) for cb in range(wl // col_w)]
    assert len(pieces) <= 3 * heads
    fill_hooks = list(range(1, len(pieces) + 1))

    def run(cur_ref, nxt_ref):
        def chunk(s, carry):
            def project_slice(hook_idx):
                if hook_idx not in fill_hooks:
                    return
                r, cb = pieces[fill_hooks.index(hook_idx)]
                rws = slice(r * REC_CHUNK, (r + 1) * REC_CHUNK)
                cols = slice(cb * col_w, (cb + 1) * col_w)
                nxt_ref[s, rws, cols] = _dot(hn_ref[rws, :], w_ref[s, :, cols])
            rows = pl.ds(pl.multiple_of(s * REC_CHUNK, REC_CHUNK), REC_CHUNK)
            _rec_chunk(cur_ref, rows, lb_all, gain_ref, sums_ref, lvl_ref, o_ref, st_ref, heads,
                       after_head=project_slice)
            return carry
        lax.fori_loop(0, N_PROJ, chunk, 0)

    @pl.when(n % 2 == 0)
    def _():
        run(pa_ref, pb_ref)

    @pl.when(n % 2 == 1)
    def _():
        run(pb_ref, pa_ref)


def _hgrn_mixer(h, w_in, hgrn_lb, out_norm_gain_l, *, layer_idx, batch, seq, layer):
    m, d = h.shape
    depth = hgrn_lb.shape[0]
    groups, wl = w_in.shape[1], w_in.shape[4]
    hb = wl // HEAD_DIM
    rb = N_PROJ * REC_CHUNK
    assert seq % rb == 0
    steps_per_seq = seq // rb
    steps_per_group = batch * steps_per_seq
    n_steps = groups * steps_per_group
    sums_np, lvl_np = _rec_constants()
    sums = jnp.asarray(sums_np, BF16)
    lvl = jnp.asarray(lvl_np)
    nxt = lambda n: jnp.minimum(n + 1, n_steps - 1)

    return pl.pallas_call(
        functools.partial(_hgrn_kernel, layer=layer, heads=hb, steps_per_seq=steps_per_seq),
        grid=(n_steps,),
        in_specs=[
            pl.BlockSpec((rb, d), lambda n: (0, 0)),
            pl.BlockSpec((rb, d), lambda n: (nxt(n) % steps_per_group, 0)),
            pl.BlockSpec((None, None, N_PROJ, d, wl), lambda n: (layer_idx, nxt(n) // steps_per_group, 0, 0, 0),
                         pipeline_mode=pl.Buffered(1)),
            pl.BlockSpec((depth, wl), lambda n: (0, n // steps_per_group)),
            pl.BlockSpec((1, wl), lambda n: (0, n // steps_per_group)),
            pl.BlockSpec(sums.shape, lambda n: (0, 0)),
            pl.BlockSpec(lvl.shape, lambda n: (0, 0)),
        ],
        out_specs=pl.BlockSpec((rb, wl), lambda n: (n % steps_per_group, n // steps_per_group)),
        out_shape=jax.ShapeDtypeStruct((m, d), BF16),
        scratch_shapes=[pltpu.VMEM((N_PROJ, rb, wl), F32), pltpu.VMEM((N_PROJ, rb, wl), F32),
                        pltpu.VMEM((hb, HEAD_DIM, HEAD_DIM), F32)],
        compiler_params=pltpu.CompilerParams(
            dimension_semantics=("arbitrary",),
            vmem_limit_bytes=V7X_VMEM_LIMIT),
        name="hgrn_proj_recurrence",
    )(h, h, w_in, hgrn_lb, out_norm_gain_l.reshape(1, d), sums, lvl)


def kernel(x, c, norm_mix_gain, norm_ffn_gain, w_ada, b_ada, pool_w, pool_scale, hgrn_w_in, hgrn_w_out,
           hgrn_norm_gain, hgrn_lb, w_ffn_in, w_ffn_out, final_gain):
    batch, seq, d = x.shape
    depth = w_ada.shape[0]
    mods = _modulation(c, w_ada, b_ada).reshape(depth, batch, N_MOD, d)
    x2 = x.reshape(batch * seq, d)
    w_in16, w_out16 = _cast_ffn_weights(w_ffn_in, w_ffn_out, 0)
    pool_w16, hgrn_out16 = pool_w.astype(BF16), hgrn_w_out.astype(BF16)
    n_heads = d // HEAD_DIM
    hb = min(REC_HEADS, n_heads)
    hgrn_in16 = hgrn_w_in.reshape(-1, d, N_PROJ, n_heads // hb, hb * HEAD_DIM)
    hgrn_in16 = hgrn_in16.transpose(0, 3, 2, 1, 4).astype(BF16)
    assert depth % 2 == 0, "layers alternate pooling / HGRN2; each pooling layer feeds the HGRN2 layer after it"
    h = None
    for layer in range(depth):
        j = layer // 2
        last = layer == depth - 1
        fin = final_gain if last else None
        cast_next = None if last else (w_ffn_in, w_ffn_out, layer + 1)
        if layer % 2 == 0:
            mixer_args = (norm_mix_gain[layer], pool_w16, pool_scale[j], j)
            x2, h, w_in16, w_out16 = _ffn_layer(
                x2, mods[layer], norm_ffn_gain[layer], w_in16, w_out16, seq=seq, mixer="pool",
                mixer_args=mixer_args, final_gain=fin, cast_next=cast_next,
                next_mixer=(mods[layer + 1], norm_mix_gain[layer + 1]))
        else:
            og = _hgrn_mixer(h, hgrn_in16, hgrn_lb, hgrn_norm_gain[j], layer_idx=j, batch=batch, seq=seq,
                             layer=layer)
            mixer_args = (og, hgrn_out16, j)
            outs = _ffn_layer(x2, mods[layer], norm_ffn_gain[layer], w_in16, w_out16, seq=seq, mixer="hgrn",
                              mixer_args=mixer_args, final_gain=fin, cast_next=cast_next)
            x2 = outs[0]
            if cast_next is not None:
                w_in16, w_out16 = outs[1:]
    return x2.reshape(batch, seq, d)
```

```python
import functools

import numpy as np
import jax
import jax.numpy as jnp
from jax import lax
from jax.experimental import pallas as pl
from jax.experimental.pallas import tpu as pltpu

EPS = 1e-6
LOG2_E = 1.4426950408889634
POOL_WINDOWS = (2, 4, 8, 16)
POOL_HALO = 16
HEAD_DIM = 128
N_MOD = 6
FFN_SUBTILES = 2
N_PROJ = 4
REC_HEADS = 8
REC_CHUNK = 128
REC_LEVELS = 7
REC_FINE_LEVELS = (4, 5)
MXU_WIDTH = 256
V7X_VMEM_LIMIT = 56 * 1024 * 1024

F32 = jnp.float32
BF16 = jnp.bfloat16


def _sigmoid(x):
    return 1.0 / (1.0 + jnp.exp(-x))


def _mod_norm(x, gain, shift, scale):
    ms = jnp.mean(x * x, axis=-1, keepdims=True)
    return x * lax.rsqrt(ms + EPS) * (gain * (1.0 + scale)) + shift


def _dot(a, b):
    return jnp.dot(a, b, preferred_element_type=F32)


def _dot_nt(a, b):
    return lax.dot_general(a, b, (((1,), (1,)), ((), ())), preferred_element_type=F32)


def _dot_tn(a, b):
    return lax.dot_general(a, b, (((0,), (0,)), ((), ())), preferred_element_type=F32)


def _mod_kernel(c_ref, w_ref, b_ref, o_ref, *, batch):
    cpad = c_ref[...]
    cond = (cpad * _sigmoid(cpad)).astype(BF16)
    res = _dot(cond, w_ref[...].astype(BF16)) + b_ref[...]
    o_ref[...] = res[:batch]


def _modulation(c, w_ada, b_ada):
    depth, d, n = w_ada.shape
    batch = c.shape[0]
    rows = 16
    cpad = jnp.zeros((rows, d), F32).at[:batch].set(c)
    tn = min(n, 1024)
    return pl.pallas_call(
        functools.partial(_mod_kernel, batch=batch),
        grid=(depth, n // tn),
        in_specs=[
            pl.BlockSpec((rows, d), lambda l, j: (0, 0)),
            pl.BlockSpec((None, d, tn), lambda l, j: (l, 0, j)),
            pl.BlockSpec((None, 1, tn), lambda l, j: (l, 0, j)),
        ],
        out_specs=pl.BlockSpec((None, batch, tn), lambda l, j: (l, 0, j)),
        out_shape=jax.ShapeDtypeStruct((depth, batch, n), F32),
        compiler_params=pltpu.CompilerParams(
            dimension_semantics=("parallel", "parallel"),
            vmem_limit_bytes=V7X_VMEM_LIMIT),
        name="adaln_mod",
    )(cpad, w_ada, b_ada.reshape(depth, 1, n))


def _pool_mixer_rows(x, r0, xh_ref, gain, shift, scale, pw_ref, ps_ref, hs_ref, first_tile, pos_base):
    n, d = x.shape
    g_dim = d // len(POOL_WINDOWS)
    if r0 == 0:
        hh = _mod_norm(xh_ref[...], gain, shift, scale)
        hs_ref[0:POOL_HALO, :] = jnp.where(first_tile, 0.0, hh)
    hs_ref[POOL_HALO + r0:POOL_HALO + r0 + n, :] = _mod_norm(x, gain, shift, scale)
    row = lax.broadcasted_iota(jnp.int32, (POOL_HALO, g_dim), 0)
    pos = (pos_base + row + 1).astype(F32)
    ys = []
    for g, w in enumerate(POOL_WINDOWS):
        lanes = slice(g * g_dim, (g + 1) * g_dim)
        s = hs_ref[r0:POOL_HALO + r0 + n, lanes]
        span = 1
        while span < w:
            s = s + pltpu.roll(s, span, axis=0)
            span *= 2
        s = s[POOL_HALO:]
        hg = hs_ref[POOL_HALO + r0:POOL_HALO + r0 + n, lanes]
        if r0 == 0:
            top = s[:POOL_HALO] / jnp.minimum(pos, float(w)) - hg[:POOL_HALO]
            rest = s[POOL_HALO:] * (1.0 / w) - hg[POOL_HALO:]
            dg = jnp.concatenate([top, rest], axis=0)
        else:
            dg = s * (1.0 / w) - hg
        ys.append(_dot(dg.astype(BF16), pw_ref[g]))
    return jnp.concatenate(ys, axis=1) * ps_ref[...]


def _cast_kernel(a_ref, b_ref, a16_ref, b16_ref):
    a16_ref[...] = a_ref[...].astype(BF16)
    b16_ref[...] = b_ref[...].astype(BF16)


def _cast_ffn_weights(w_in, w_out, layer, steps=16):
    _, d, n2 = w_in.shape
    _, d_ff, _ = w_out.shape
    ra, rb = d // steps, d_ff // steps
    assert ra * steps == d and rb * steps == d_ff and ra % 16 == 0 and rb % 16 == 0
    return pl.pallas_call(
        _cast_kernel,
        grid=(steps,),
        in_specs=[pl.BlockSpec((None, ra, n2), lambda s: (layer, s, 0)),
                  pl.BlockSpec((None, rb, d), lambda s: (layer, s, 0))],
        out_specs=[pl.BlockSpec((ra, n2), lambda s: (s, 0)), pl.BlockSpec((rb, d), lambda s: (s, 0))],
        out_shape=[jax.ShapeDtypeStruct((d, n2), BF16), jax.ShapeDtypeStruct((d_ff, d), BF16)],
        compiler_params=pltpu.CompilerParams(
            dimension_semantics=("parallel",), vmem_limit_bytes=V7X_VMEM_LIMIT),
        name="cast_ffn_weights",
    )(w_in, w_out)


def _ffn_kernel(*refs, mixer, final, feeds_next, n_casts, nf, tiles_per_seq, tm):
    if mixer == "pool":
        (x_ref, xh_ref, mod_ref, gm_ref, gf_ref, pw_ref, ps_ref, wa_ref, wb_ref, wo_ref) = refs[:10]
        rest = refs[10:]
    else:
        (x_ref, og_ref, mod_ref, gf_ref, wout_ref, wa_ref, wb_ref, wo_ref) = refs[:8]
        rest = refs[8:]
    if final:
        fg_ref, rest = rest[0], rest[1:]
    if feeds_next:
        modn_ref, gn_ref, rest = rest[0], rest[1], rest[2:]
    cast_src, rest = rest[:n_casts], rest[n_casts:]
    o_ref, rest = rest[0], rest[1:]
    if feeds_next:
        hn_ref, rest = rest[0], rest[1:]
    cast_dst, rest = rest[:n_casts], rest[n_casts:]
    if mixer == "pool":
        h2_ref, acc_ref, hs_ref = rest
    else:
        h2_ref, acc_ref = rest

    i = pl.program_id(0)
    f = pl.program_id(1)
    tile_in_seq = i % tiles_per_seq

    def cast_slabs():
        for src, dst in zip(cast_src, cast_dst):
            dst[...] = src[...].astype(BF16)

    def ffn_chunk(h2):
        n = h2.shape[0]
        halves = [h2] if n < 2 * MXU_WIDTH else [h2[:n // 2], h2[n // 2:]]
        us = []
        for hh in halves:
            a = _dot(hh, wa_ref[...])
            b = _dot(hh, wb_ref[...])
            us.append((a * _sigmoid(a) * b).astype(BF16))
        return jnp.concatenate([_dot(u, wo_ref[...]) for u in us], axis=0)

    @pl.when(f == 0)
    def _():
        cast_slabs()
        sub = tm // FFN_SUBTILES
        if mixer == "hgrn":
            ys = [_dot(og_ref[r0:r0 + sub, :], wout_ref[...]) for r0 in range(0, tm, sub)]
        for t, r0 in enumerate(range(0, tm, sub)):
            rows = slice(r0, r0 + sub)
            x = x_ref[rows, :]
            if mixer == "pool":
                y = _pool_mixer_rows(x, r0, xh_ref, gm_ref[...], mod_ref[0:1, :], mod_ref[1:2, :],
                                     pw_ref, ps_ref, hs_ref, tile_in_seq == 0, tile_in_seq * tm)
            else:
                y = ys[t]
            xm = x + mod_ref[2:3, :] * y
            o_ref[rows, :] = xm
            h2 = _mod_norm(xm, gf_ref[...], mod_ref[3:4, :], mod_ref[4:5, :]).astype(BF16)
            h2_ref[rows, :] = h2
            acc_ref[rows, :] = ffn_chunk(h2)

    @pl.when(jnp.logical_and(f > 0, f < nf - 1))
    def _():
        cast_slabs()
        acc_ref[...] += ffn_chunk(h2_ref[...])

    @pl.when(f == nf - 1)
    def _():
        cast_slabs()
        sub = tm // FFN_SUBTILES
        for r0 in range(0, tm, sub):
            rows = slice(r0, r0 + sub)
            acc = acc_ref[rows, :] + ffn_chunk(h2_ref[rows, :])
            out = o_ref[rows, :] + mod_ref[5:6, :] * acc
            if final:
                ms = jnp.mean(out * out, axis=-1, keepdims=True)
                out = out * lax.rsqrt(ms + EPS) * fg_ref[...]
            o_ref[rows, :] = out
            if feeds_next:
                hn = _mod_norm(out, gn_ref[...], modn_ref[0:1, :], modn_ref[1:2, :])
                hn_ref[rows, :] = hn.astype(BF16)


def _ffn_layer(x2, mod_l, norm_ffn_gain_l, w_in, w_out, *, seq, mixer, mixer_args, final_gain=None,
               next_mixer=None, cast_next=None, cast_hgrn=None, tm=512, tf=512):
    m, d = x2.shape
    d_ff = w_out.shape[0]
    tm = min(tm, seq)
    tf = min(tf, d_ff)
    assert seq % tm == 0 and d_ff % tf == 0 and tm % (FFN_SUBTILES * POOL_HALO) == 0
    nf = d_ff // tf
    assert nf >= 2, "the first and the last d_ff step are distinct code paths"
    tiles_per_seq = seq // tm
    row = lambda v: v.reshape(1, d)
    x_spec = pl.BlockSpec((tm, d), lambda i, f: (i, 0))
    mod_spec = pl.BlockSpec((None, N_MOD, d), lambda i, f: (i // tiles_per_seq, 0, 0))
    vec_spec = pl.BlockSpec((1, d), lambda i, f: (0, 0))
    ffn_specs = [
        pl.BlockSpec((d, tf), lambda i, f: (0, f)),
        pl.BlockSpec((d, tf), lambda i, f: (0, nf + f)),
        pl.BlockSpec((tf, d), lambda i, f: (f, 0)),
    ]
    scratch = [pltpu.VMEM((tm, d), BF16), pltpu.VMEM((tm, d), F32)]
    if mixer == "pool":
        norm_mix_gain_l, pool_w, pool_scale, j = mixer_args
        _, n_groups, g_dim, _ = pool_w.shape
        halo_blocks = tm // POOL_HALO
        args = [x2, x2, mod_l, row(norm_mix_gain_l), row(norm_ffn_gain_l), pool_w, row(pool_scale)]
        specs = [x_spec,
                 pl.BlockSpec((POOL_HALO, d), lambda i, f: (jnp.maximum(i * halo_blocks - 1, 0), 0)),
                 mod_spec, vec_spec, vec_spec,
                 pl.BlockSpec((None, n_groups, g_dim, g_dim), lambda i, f: (j, 0, 0, 0)),
                 vec_spec]
        scratch.append(pltpu.VMEM((tm + POOL_HALO, d), F32))
    else:
        og, w_mix_out, j = mixer_args
        args = [x2, og, mod_l, row(norm_ffn_gain_l), w_mix_out]
        specs = [x_spec, pl.BlockSpec((tm, d), lambda i, f: (i, 0)), mod_spec, vec_spec,
                 pl.BlockSpec((None, d, d), lambda i, f: (j, 0, 0), pipeline_mode=pl.Buffered(1))]
    args += [w_in, w_in, w_out]
    specs += ffn_specs
    if final_gain is not None:
        args.append(row(final_gain))
        specs.append(vec_spec)
    out_specs = [pl.BlockSpec((tm, d), lambda i, f: (i, 0))]
    out_shape = [jax.ShapeDtypeStruct((m, d), F32)]
    if next_mixer is not None:
        args += [next_mixer[0], row(next_mixer[1])]
        specs += [mod_spec, vec_spec]
        out_specs.append(pl.BlockSpec((tm, d), lambda i, f: (i, 0)))
        out_shape.append(jax.ShapeDtypeStruct((m, d), BF16))
    n_i = m // tm
    cast_in, cast_in_specs, cast_out_specs, cast_out_shape = [], [], [], []
    if cast_next is not None:
        w_in32, w_out32, nxt = cast_next
        ri, ci, ro = d // n_i, 2 * d_ff // nf, d_ff // (n_i * nf)
        assert ri * n_i == d and ci * nf == 2 * d_ff and ro * n_i * nf == d_ff
        assert ri % 16 == 0 and ro % 16 == 0 and ci % 128 == 0
        cast_in += [w_in32, w_out32]
        cast_in_specs += [pl.BlockSpec((None, ri, ci), lambda i, f: (nxt, i, f)),
                          pl.BlockSpec((None, ro, d), lambda i, f: (nxt, i * nf + f, 0))]
        cast_out_specs += [pl.BlockSpec((ri, ci), lambda i, f: (i, f)),
                           pl.BlockSpec((ro, d), lambda i, f: (i * nf + f, 0))]
        cast_out_shape += [jax.ShapeDtypeStruct((d, 2 * d_ff), BF16), jax.ShapeDtypeStruct((d_ff, d), BF16)]
    if cast_hgrn is not None:
        hw32, jn, wl = cast_hgrn
        ncb = hw32.shape[2] // wl
        groups = ncb // N_PROJ
        row_blocks = max(rbk for rbk in (1, 2, 4, 8, 16, 32, 64, 128)
                         if rbk * ncb <= n_i * nf and d % rbk == 0 and (d // rbk) % 16 == 0)
        rh = d // row_blocks
        slab = lambda i, f: jnp.minimum(i * nf + f, row_blocks * ncb - 1)
        cast_in.append(hw32)
        cast_in_specs.append(pl.BlockSpec((None, rh, wl), lambda i, f: (jn, slab(i, f) // ncb, slab(i, f) % ncb)))
        cast_out_specs.append(pl.BlockSpec(
            (None, None, rh, wl),
            lambda i, f: ((slab(i, f) % ncb) % groups, (slab(i, f) % ncb) // groups, slab(i, f) // ncb, 0)))
        cast_out_shape.append(jax.ShapeDtypeStruct((groups, N_PROJ, d, wl), BF16))
    args += cast_in
    specs += cast_in_specs
    out_specs += cast_out_specs
    out_shape += cast_out_shape
    return pl.pallas_call(
        functools.partial(_ffn_kernel, mixer=mixer, final=final_gain is not None,
                          feeds_next=next_mixer is not None, n_casts=len(cast_in),
                          nf=nf, tiles_per_seq=tiles_per_seq, tm=tm),
        grid=(m // tm, nf),
        in_specs=specs,
        out_specs=out_specs,
        out_shape=out_shape,
        scratch_shapes=scratch,
        compiler_params=pltpu.CompilerParams(
            dimension_semantics=("arbitrary", "arbitrary"),
            vmem_limit_bytes=V7X_VMEM_LIMIT),
        name="mixer_out_ffn_" + mixer,
    )(*args)


def _level_split(l):
    n = REC_CHUNK >> l
    return n, n // 2


def _rec_constants():
    c = REC_CHUNK
    t = np.arange(c)[:, None]
    u = np.arange(c)[None, :]
    mats = [u <= t]
    level = np.full((c, c), -1, np.int32)
    for l in range(REC_LEVELS):
        n, half = _level_split(l)
        mid = (t // n) * n + half
        if l in REC_FINE_LEVELS:
            mats.append(np.where(t >= mid, (u >= mid) & (u <= t), (u > t) & (u < mid)))
        same = (t // n) == (u // n)
        level[same & (t % n >= half) & (u % n < half)] = l
    level[np.arange(c), np.arange(c)] = REC_LEVELS
    return np.concatenate(mats, axis=0).astype(np.float32), level


def _coarse_level_operand(q, k, cum, l):
    n, half = _level_split(l)
    pieces = []
    for lo in range(0, REC_CHUNK, n):
        mid, hi = lo + half, lo + n
        ref = cum[mid - 1:mid]
        pieces.append(k[lo:mid] * jnp.exp2(ref - cum[lo:mid]))
        pieces.append(q[mid:hi] * jnp.exp2(cum[mid:hi] - ref))
    return jnp.concatenate(pieces, axis=0)


def _rec_chunk(proj_ref, rows, lb_all, gain_ref, sums_ref, lvl_ref, o_ref, st_ref, heads, after_head=None):
    c = REC_CHUNK
    lvl = lvl_ref[...]
    in_level = [lvl == l for l in range(REC_LEVELS + 1)]
    rowi = lax.broadcasted_iota(jnp.int32, (c, HEAD_DIM), 0)
    second_half = {l: (rowi & _level_split(l)[1]) != 0 for l in REC_FINE_LEVELS + (REC_LEVELS - 1,)}
    sums = sums_ref[...]
    hook = iter(range(3 * heads + 1))

    def fill():
        if after_head is not None:
            after_head(next(hook))

    fill()

    gates, parts = [], []
    for j in range(heads):
        lanes = slice(j * HEAD_DIM, (j + 1) * HEAD_DIM)
        lb = lb_all[:, lanes]
        fgt = lb + (1.0 - lb) * _sigmoid(proj_ref[1, rows, lanes])
        logf = jnp.log(fgt) * LOG2_E
        hi = logf.astype(BF16)
        lo = (logf - hi.astype(F32)).astype(BF16)
        gates.append(fgt)
        parts.append(_dot(sums, jnp.concatenate([hi, lo], axis=1)))
        fill()

    all_scores, cums = [], []
    for j in range(heads):
        lanes = slice(j * HEAD_DIM, (j + 1) * HEAD_DIM)
        q = proj_ref[0, rows, lanes]
        fgt = gates[j]
        k = 1.0 - fgt
        ex = parts[j][:, :HEAD_DIM] + parts[j][:, HEAD_DIM:]
        cum = ex[:c]
        scores = jnp.zeros((c, c), F32)
        for l in range(REC_LEVELS):
            if l in REC_FINE_LEVELS:
                i = 1 + REC_FINE_LEVELS.index(l)
                xl = jnp.where(second_half[l], q, k) * jnp.exp2(ex[i * c:(i + 1) * c])
            elif l == REC_LEVELS - 1:
                xl = jnp.where(second_half[l], q * fgt, k)
            else:
                xl = _coarse_level_operand(q, k, cum, l)
            xt = xl.T.astype(BF16)
            xl = xl.astype(BF16)
            scores = jnp.where(in_level[l], _dot(xl, xt), scores)
        scores = jnp.where(in_level[REC_LEVELS], jnp.sum(q * k, axis=-1, keepdims=True), scores)
        all_scores.append(scores.astype(BF16))
        cums.append(cum)
        fill()

    for j in range(heads):
        lanes = slice(j * HEAD_DIM, (j + 1) * HEAD_DIM)
        q = proj_ref[0, rows, lanes]
        v = proj_ref[2, rows, lanes].astype(BF16)
        gate = proj_ref[3, rows, lanes]
        k = 1.0 - gates[j]
        cum = cums[j]
        st = st_ref[j]
        end = cum[c - 1:c]
        qd = (q * jnp.exp2(cum)).astype(BF16)
        o = _dot(all_scores[j], v) + _dot(qd, st.T.astype(BF16))
        kd = (k * jnp.exp2(end - cum)).astype(BF16)
        st_ref[j] = st * jnp.exp2(end) + _dot_tn(v, kd)

        ms = jnp.mean(o * o, axis=-1, keepdims=True)
        on = o * lax.rsqrt(ms + EPS) * gain_ref[:, lanes] * (gate * _sigmoid(gate))
        o_ref[rows, lanes] = on.astype(BF16)
        fill()


def _hgrn_kernel(h0_ref, hn_ref, w_ref, lbp_ref, gain_ref, sums_ref, lvl_ref, o_ref,
                 pa_ref, pb_ref, st_ref, *, layer, heads, steps_per_seq):
    n = pl.program_id(0)

    @pl.when(n == 0)
    def _():
        for part in range(N_PROJ):
            pa_ref[part] = _dot(h0_ref[...], w_ref[part])

    @pl.when(n % steps_per_seq == 0)
    def _():
        st_ref[...] = jnp.zeros_like(st_ref)

    lbp = lbp_ref[...]
    e = jnp.exp(lbp - jnp.max(lbp, axis=0, keepdims=True))
    p = e / jnp.sum(e, axis=0, keepdims=True)
    lb_all = jnp.sum(p[1:layer + 1], axis=0, keepdims=True) if layer > 0 else jnp.zeros_like(p[0:1])

    wl = heads * HEAD_DIM
    col_w = min(MXU_WIDTH, wl)
    pieces = [(r, cb) for r in range(N_PROJ) for cb in range(wl // col_w)]
    assert len(pieces) <= 3 * heads
    first = min(heads + 1, 3 * heads + 1 - len(pieces))
    fill_hooks = list(range(first, first + len(pieces)))

    def run(cur_ref, nxt_ref):
        def chunk(s, carry):
            def project_slice(hook_idx):
                if hook_idx not in fill_hooks:
                    return
                r, cb = pieces[fill_hooks.index(hook_idx)]
                rws = slice(r * REC_CHUNK, (r + 1) * REC_CHUNK)
                cols = slice(cb * col_w, (cb + 1) * col_w)
                nxt_ref[s, rws, cols] = _dot(hn_ref[rws, :], w_ref[s, :, cols])
            rows = pl.ds(pl.multiple_of(s * REC_CHUNK, REC_CHUNK), REC_CHUNK)
            _rec_chunk(cur_ref, rows, lb_all, gain_ref, sums_ref, lvl_ref, o_ref, st_ref, heads,
                       after_head=project_slice)
            return carry
        lax.fori_loop(0, N_PROJ, chunk, 0)

    @pl.when(n % 2 == 0)
    def _():
        run(pa_ref, pb_ref)

    @pl.when(n % 2 == 1)
    def _():
        run(pb_ref, pa_ref)


def _hgrn_mixer(h, w_in, hgrn_lb, out_norm_gain_l, *, batch, seq, layer):
    m, d = h.shape
    depth = hgrn_lb.shape[0]
    groups, wl = w_in.shape[0], w_in.shape[3]
    hb = wl // HEAD_DIM
    rb = N_PROJ * REC_CHUNK
    assert seq % rb == 0
    steps_per_seq = seq // rb
    steps_per_group = batch * steps_per_seq
    n_steps = groups * steps_per_group
    sums_np, lvl_np = _rec_constants()
    sums = jnp.asarray(sums_np, BF16)
    lvl = jnp.asarray(lvl_np)
    nxt = lambda n: jnp.minimum(n + 1, n_steps - 1)

    return pl.pallas_call(
        functools.partial(_hgrn_kernel, layer=layer, heads=hb, steps_per_seq=steps_per_seq),
        grid=(n_steps,),
        in_specs=[
            pl.BlockSpec((rb, d), lambda n: (0, 0)),
            pl.BlockSpec((rb, d), lambda n: (nxt(n) % steps_per_group, 0)),
            pl.BlockSpec((None, N_PROJ, d, wl), lambda n: (nxt(n) // steps_per_group, 0, 0, 0),
                         pipeline_mode=pl.Buffered(1)),
            pl.BlockSpec((depth, wl), lambda n: (0, n // steps_per_group)),
            pl.BlockSpec((1, wl), lambda n: (0, n // steps_per_group)),
            pl.BlockSpec(sums.shape, lambda n: (0, 0)),
            pl.BlockSpec(lvl.shape, lambda n: (0, 0)),
        ],
        out_specs=pl.BlockSpec((rb, wl), lambda n: (n % steps_per_group, n // steps_per_group)),
        out_shape=jax.ShapeDtypeStruct((m, d), BF16),
        scratch_shapes=[pltpu.VMEM((N_PROJ, rb, wl), F32), pltpu.VMEM((N_PROJ, rb, wl), F32),
                        pltpu.VMEM((hb, HEAD_DIM, HEAD_DIM), F32)],
        compiler_params=pltpu.CompilerParams(
            dimension_semantics=("arbitrary",),
            vmem_limit_bytes=V7X_VMEM_LIMIT),
        name="hgrn_proj_recurrence",
    )(h, h, w_in, hgrn_lb, out_norm_gain_l.reshape(1, d), sums, lvl)


def kernel(x, c, norm_mix_gain, norm_ffn_gain, w_ada, b_ada, pool_w, pool_scale, hgrn_w_in, hgrn_w_out,
           hgrn_norm_gain, hgrn_lb, w_ffn_in, w_ffn_out, final_gain):
    batch, seq, d = x.shape
    depth = w_ada.shape[0]
    mods = _modulation(c, w_ada, b_ada).reshape(depth, batch, N_MOD, d)
    x2 = x.reshape(batch * seq, d)
    w_in16, w_out16 = _cast_ffn_weights(w_ffn_in, w_ffn_out, 0)
    pool_w16, hgrn_out16 = pool_w.astype(BF16), hgrn_w_out.astype(BF16)
    group_width = min(REC_HEADS, d // HEAD_DIM) * HEAD_DIM
    assert depth % 2 == 0, "layers alternate pooling / HGRN2; each pooling layer feeds the HGRN2 layer after it"
    h = hgrn_in16 = None
    for layer in range(depth):
        j = layer // 2
        last = layer == depth - 1
        fin = final_gain if last else None
        cast_next = None if last else (w_ffn_in, w_ffn_out, layer + 1)
        if layer % 2 == 0:
            mixer_args = (norm_mix_gain[layer], pool_w16, pool_scale[j], j)
            x2, h, w_in16, w_out16, hgrn_in16 = _ffn_layer(
                x2, mods[layer], norm_ffn_gain[layer], w_in16, w_out16, seq=seq, mixer="pool",
                mixer_args=mixer_args, final_gain=fin, cast_next=cast_next,
                cast_hgrn=(hgrn_w_in, j, group_width),
                next_mixer=(mods[layer + 1], norm_mix_gain[layer + 1]))
        else:
            og = _hgrn_mixer(h, hgrn_in16, hgrn_lb, hgrn_norm_gain[j], batch=batch, seq=seq, layer=layer)
            mixer_args = (og, hgrn_out16, j)
            outs = _ffn_layer(x2, mods[layer], norm_ffn_gain[layer], w_in16, w_out16, seq=seq, mixer="hgrn",
                              mixer_args=mixer_args, final_gain=fin, cast_next=cast_next)
            x2 = outs[0]
            if cast_next is not None:
                w_in16, w_out16 = outs[1:]
    return x2.reshape(batch, seq, d)
```

```python
import functools

import numpy as np
import jax
import jax.numpy as jnp
from jax import lax
from jax.experimental import pallas as pl
from jax.experimental.pallas import tpu as pltpu

EPS = 1e-6
LOG2_E = 1.4426950408889634
POOL_WINDOWS = (2, 4, 8, 16)
POOL_HALO = 16
HEAD_DIM = 128
N_MOD = 6
FFN_SUBTILES = 2
N_PROJ = 4
REC_HEADS = 8
REC_CHUNK = 128
REC_LEVELS = 7
REC_FINE_LEVELS = (4, 5)
MXU_WIDTH = 256
V7X_VMEM_LIMIT = 56 * 1024 * 1024

F32 = jnp.float32
BF16 = jnp.bfloat16


def _sigmoid(x):
    return 1.0 / (1.0 + jnp.exp(-x))


def _mod_norm(x, gain, shift, scale):
    ms = jnp.mean(x * x, axis=-1, keepdims=True)
    return x * lax.rsqrt(ms + EPS) * (gain * (1.0 + scale)) + shift


def _dot(a, b):
    return jnp.dot(a, b, preferred_element_type=F32)


def _dot_nt(a, b):
    return lax.dot_general(a, b, (((1,), (1,)), ((), ())), preferred_element_type=F32)


def _dot_tn(a, b):
    return lax.dot_general(a, b, (((0,), (0,)), ((), ())), preferred_element_type=F32)


def _mod_kernel(c_ref, w_ref, b_ref, o_ref, *, batch):
    cpad = c_ref[...]
    cond = (cpad * _sigmoid(cpad)).astype(BF16)
    res = _dot(cond, w_ref[...].astype(BF16)) + b_ref[...]
    o_ref[...] = res[:batch]


def _modulation(c, w_ada, b_ada):
    depth, d, n = w_ada.shape
    batch = c.shape[0]
    rows = 16
    cpad = jnp.zeros((rows, d), F32).at[:batch].set(c)
    tn = min(n, 1024)
    return pl.pallas_call(
        functools.partial(_mod_kernel, batch=batch),
        grid=(depth, n // tn),
        in_specs=[
            pl.BlockSpec((rows, d), lambda l, j: (0, 0)),
            pl.BlockSpec((None, d, tn), lambda l, j: (l, 0, j)),
            pl.BlockSpec((None, 1, tn), lambda l, j: (l, 0, j)),
        ],
        out_specs=pl.BlockSpec((None, batch, tn), lambda l, j: (l, 0, j)),
        out_shape=jax.ShapeDtypeStruct((depth, batch, n), F32),
        compiler_params=pltpu.CompilerParams(
            dimension_semantics=("parallel", "parallel"),
            vmem_limit_bytes=V7X_VMEM_LIMIT),
        name="adaln_mod",
    )(cpad, w_ada, b_ada.reshape(depth, 1, n))


def _pool_mixer_rows(x, r0, xh_ref, gain, shift, scale, pw_ref, ps_ref, hs_ref, first_tile, pos_base):
    n, d = x.shape
    g_dim = d // len(POOL_WINDOWS)
    if r0 == 0:
        hh = _mod_norm(xh_ref[...], gain, shift, scale)
        hs_ref[0:POOL_HALO, :] = jnp.where(first_tile, 0.0, hh)
    hs_ref[POOL_HALO + r0:POOL_HALO + r0 + n, :] = _mod_norm(x, gain, shift, scale)
    row = lax.broadcasted_iota(jnp.int32, (POOL_HALO, g_dim), 0)
    pos = (pos_base + row + 1).astype(F32)
    ys = []
    for g, w in enumerate(POOL_WINDOWS):
        lanes = slice(g * g_dim, (g + 1) * g_dim)
        s = hs_ref[r0:POOL_HALO + r0 + n, lanes]
        span = 1
        while span < w:
            s = s + pltpu.roll(s, span, axis=0)
            span *= 2
        s = s[POOL_HALO:]
        hg = hs_ref[POOL_HALO + r0:POOL_HALO + r0 + n, lanes]
        if r0 == 0:
            top = s[:POOL_HALO] / jnp.minimum(pos, float(w)) - hg[:POOL_HALO]
            rest = s[POOL_HALO:] * (1.0 / w) - hg[POOL_HALO:]
            dg = jnp.concatenate([top, rest], axis=0)
        else:
            dg = s * (1.0 / w) - hg
        ys.append(_dot(dg.astype(BF16), pw_ref[g]))
    return jnp.concatenate(ys, axis=1) * ps_ref[...]


def _cast_kernel(a_ref, b_ref, a16_ref, b16_ref):
    a16_ref[...] = a_ref[...].astype(BF16)
    b16_ref[...] = b_ref[...].astype(BF16)


def _cast_ffn_weights(w_in, w_out, layer, steps=16):
    _, d, n2 = w_in.shape
    _, d_ff, _ = w_out.shape
    ra, rb = d // steps, d_ff // steps
    assert ra * steps == d and rb * steps == d_ff and ra % 16 == 0 and rb % 16 == 0
    return pl.pallas_call(
        _cast_kernel,
        grid=(steps,),
        in_specs=[pl.BlockSpec((None, ra, n2), lambda s: (layer, s, 0)),
                  pl.BlockSpec((None, rb, d), lambda s: (layer, s, 0))],
        out_specs=[pl.BlockSpec((ra, n2), lambda s: (s, 0)), pl.BlockSpec((rb, d), lambda s: (s, 0))],
        out_shape=[jax.ShapeDtypeStruct((d, n2), BF16), jax.ShapeDtypeStruct((d_ff, d), BF16)],
        compiler_params=pltpu.CompilerParams(
            dimension_semantics=("parallel",), vmem_limit_bytes=V7X_VMEM_LIMIT),
        name="cast_ffn_weights",
    )(w_in, w_out)


def _ffn_kernel(*refs, mixer, final, feeds_next, n_casts, nf, tiles_per_seq, tm):
    if mixer == "pool":
        (x_ref, xh_ref, mod_ref, gm_ref, gf_ref, pw_ref, ps_ref, wa_ref, wb_ref, wo_ref) = refs[:10]
        rest = refs[10:]
    else:
        (x_ref, og_ref, mod_ref, gf_ref, wout_ref, wa_ref, wb_ref, wo_ref) = refs[:8]
        rest = refs[8:]
    if final:
        fg_ref, rest = rest[0], rest[1:]
    if feeds_next:
        modn_ref, gn_ref, rest = rest[0], rest[1], rest[2:]
    cast_src, rest = rest[:n_casts], rest[n_casts:]
    o_ref, rest = rest[0], rest[1:]
    if feeds_next:
        hn_ref, rest = rest[0], rest[1:]
    cast_dst, rest = rest[:n_casts], rest[n_casts:]
    if mixer == "pool":
        h2_ref, acc_ref, hs_ref = rest
    else:
        h2_ref, acc_ref = rest

    i = pl.program_id(0)
    f = pl.program_id(1)
    tile_in_seq = i % tiles_per_seq

    def cast_slabs():
        for src, dst in zip(cast_src, cast_dst):
            dst[...] = src[...].astype(BF16)

    def ffn_chunk(h2):
        n = h2.shape[0]
        halves = [h2] if n < 2 * MXU_WIDTH else [h2[:n // 2], h2[n // 2:]]
        us = []
        for hh in halves:
            a = _dot(hh, wa_ref[...])
            b = _dot(hh, wb_ref[...])
            us.append((a * _sigmoid(a) * b).astype(BF16))
        return jnp.concatenate([_dot(u, wo_ref[...]) for u in us], axis=0)

    @pl.when(f == 0)
    def _():
        cast_slabs()
        sub = tm // FFN_SUBTILES
        if mixer == "hgrn":
            ys = [_dot(og_ref[r0:r0 + sub, :], wout_ref[...]) for r0 in range(0, tm, sub)]
        for t, r0 in enumerate(range(0, tm, sub)):
            rows = slice(r0, r0 + sub)
            x = x_ref[rows, :]
            if mixer == "pool":
                y = _pool_mixer_rows(x, r0, xh_ref, gm_ref[...], mod_ref[0:1, :], mod_ref[1:2, :],
                                     pw_ref, ps_ref, hs_ref, tile_in_seq == 0, tile_in_seq * tm)
            else:
                y = ys[t]
            xm = x + mod_ref[2:3, :] * y
            o_ref[rows, :] = xm
            h2 = _mod_norm(xm, gf_ref[...], mod_ref[3:4, :], mod_ref[4:5, :]).astype(BF16)
            h2_ref[rows, :] = h2
            acc_ref[rows, :] = ffn_chunk(h2)

    @pl.when(jnp.logical_and(f > 0, f < nf - 1))
    def _():
        cast_slabs()
        acc_ref[...] += ffn_chunk(h2_ref[...])

    @pl.when(f == nf - 1)
    def _():
        cast_slabs()
        sub = tm // FFN_SUBTILES
        for r0 in range(0, tm, sub):
            rows = slice(r0, r0 + sub)
            acc = acc_ref[rows, :] + ffn_chunk(h2_ref[rows, :])
            out = o_ref[rows, :] + mod_ref[5:6, :] * acc
            if final:
                ms = jnp.mean(out * out, axis=-1, keepdims=True)
                out = out * lax.rsqrt(ms + EPS) * fg_ref[...]
            o_ref[rows, :] = out
            if feeds_next:
                hn = _mod_norm(out, gn_ref[...], modn_ref[0:1, :], modn_ref[1:2, :])
                hn_ref[rows, :] = hn.astype(BF16)


def _ffn_layer(x2, mod_l, norm_ffn_gain_l, w_in, w_out, *, seq, mixer, mixer_args, final_gain=None,
               next_mixer=None, cast_next=None, cast_hgrn=None, tm=512, tf=512):
    m, d = x2.shape
    d_ff = w_out.shape[0]
    tm = min(tm, seq)
    tf = min(tf, d_ff)
    assert seq % tm == 0 and d_ff % tf == 0 and tm % (FFN_SUBTILES * POOL_HALO) == 0
    nf = d_ff // tf
    assert nf >= 2, "the first and the last d_ff step are distinct code paths"
    tiles_per_seq = seq // tm
    row = lambda v: v.reshape(1, d)
    x_spec = pl.BlockSpec((tm, d), lambda i, f: (i, 0))
    mod_spec = pl.BlockSpec((None, N_MOD, d), lambda i, f: (i // tiles_per_seq, 0, 0))
    vec_spec = pl.BlockSpec((1, d), lambda i, f: (0, 0))
    ffn_specs = [
        pl.BlockSpec((d, tf), lambda i, f: (0, f)),
        pl.BlockSpec((d, tf), lambda i, f: (0, nf + f)),
        pl.BlockSpec((tf, d), lambda i, f: (f, 0)),
    ]
    scratch = [pltpu.VMEM((tm, d), BF16), pltpu.VMEM((tm, d), F32)]
    if mixer == "pool":
        norm_mix_gain_l, pool_w, pool_scale, j = mixer_args
        _, n_groups, g_dim, _ = pool_w.shape
        halo_blocks = tm // POOL_HALO
        args = [x2, x2, mod_l, row(norm_mix_gain_l), row(norm_ffn_gain_l), pool_w, row(pool_scale)]
        specs = [x_spec,
                 pl.BlockSpec((POOL_HALO, d), lambda i, f: (jnp.maximum(i * halo_blocks - 1, 0), 0)),
                 mod_spec, vec_spec, vec_spec,
                 pl.BlockSpec((None, n_groups, g_dim, g_dim), lambda i, f: (j, 0, 0, 0)),
                 vec_spec]
        scratch.append(pltpu.VMEM((tm + POOL_HALO, d), F32))
    else:
        og, w_mix_out, j = mixer_args
        args = [x2, og, mod_l, row(norm_ffn_gain_l), w_mix_out]
        specs = [x_spec, pl.BlockSpec((tm, d), lambda i, f: (i, 0)), mod_spec, vec_spec,
                 pl.BlockSpec((None, d, d), lambda i, f: (j, 0, 0), pipeline_mode=pl.Buffered(1))]
    args += [w_in, w_in, w_out]
    specs += ffn_specs
    if final_gain is not None:
        args.append(row(final_gain))
        specs.append(vec_spec)
    out_specs = [pl.BlockSpec((tm, d), lambda i, f: (i, 0))]
    out_shape = [jax.ShapeDtypeStruct((m, d), F32)]
    if next_mixer is not None:
        args += [next_mixer[0], row(next_mixer[1])]
        specs += [mod_spec, vec_spec]
        out_specs.append(pl.BlockSpec((tm, d), lambda i, f: (i, 0)))
        out_shape.append(jax.ShapeDtypeStruct((m, d), BF16))
    n_i = m // tm
    cast_in, cast_in_specs, cast_out_specs, cast_out_shape = [], [], [], []
    if cast_next is not None:
        w_in32, w_out32, nxt = cast_next
        ri, ci, ro = d // n_i, 2 * d_ff // nf, d_ff // (n_i * nf)
        assert ri * n_i == d and ci * nf == 2 * d_ff and ro * n_i * nf == d_ff
        assert ri % 16 == 0 and ro % 16 == 0 and ci % 128 == 0
        cast_in += [w_in32, w_out32]
        cast_in_specs += [pl.BlockSpec((None, ri, ci), lambda i, f: (nxt, i, f)),
                          pl.BlockSpec((None, ro, d), lambda i, f: (nxt, i * nf + f, 0))]
        cast_out_specs += [pl.BlockSpec((ri, ci), lambda i, f: (i, f)),
                           pl.BlockSpec((ro, d), lambda i, f: (i * nf + f, 0))]
        cast_out_shape += [jax.ShapeDtypeStruct((d, 2 * d_ff), BF16), jax.ShapeDtypeStruct((d_ff, d), BF16)]
    if cast_hgrn is not None:
        hw32, jn, wl = cast_hgrn
        ncb = hw32.shape[2] // wl
        groups = ncb // N_PROJ
        row_blocks = max(rbk for rbk in (1, 2, 4, 8, 16, 32, 64, 128)
                         if rbk * ncb <= n_i * nf and d % rbk == 0 and (d // rbk) % 16 == 0)
        rh = d // row_blocks
        slab = lambda i, f: jnp.minimum(i * nf + f, row_blocks * ncb - 1)
        cast_in.append(hw32)
        cast_in_specs.append(pl.BlockSpec((None, rh, wl), lambda i, f: (jn, slab(i, f) // ncb, slab(i, f) % ncb)))
        cast_out_specs.append(pl.BlockSpec(
            (None, None, rh, wl),
            lambda i, f: ((slab(i, f) % ncb) % groups, (slab(i, f) % ncb) // groups, slab(i, f) // ncb, 0)))
        cast_out_shape.append(jax.ShapeDtypeStruct((groups, N_PROJ, d, wl), BF16))
    args += cast_in
    specs += cast_in_specs
    out_specs += cast_out_specs
    out_shape += cast_out_shape
    return pl.pallas_call(
        functools.partial(_ffn_kernel, mixer=mixer, final=final_gain is not None,
                          feeds_next=next_mixer is not None, n_casts=len(cast_in),
                          nf=nf, tiles_per_seq=tiles_per_seq, tm=tm),
        grid=(m // tm, nf),
        in_specs=specs,
        out_specs=out_specs,
        out_shape=out_shape,
        scratch_shapes=scratch,
        compiler_params=pltpu.CompilerParams(
            dimension_semantics=("arbitrary", "arbitrary"),
            vmem_limit_bytes=V7X_VMEM_LIMIT),
        name="mixer_out_ffn_" + mixer,
    )(*args)


def _level_split(l):
    n = REC_CHUNK >> l
    return n, n // 2


def _rec_constants():
    c = REC_CHUNK
    t = np.arange(c)[:, None]
    u = np.arange(c)[None, :]
    mats = [u <= t]
    level = np.full((c, c), -1, np.int32)
    for l in range(REC_LEVELS):
        n, half = _level_split(l)
        mid = (t // n) * n + half
        if l in REC_FINE_LEVELS:
            mats.append(np.where(t >= mid, (u >= mid) & (u <= t), (u > t) & (u < mid)))
        same = (t // n) == (u // n)
        level[same & (t % n >= half) & (u % n < half)] = l
    level[np.arange(c), np.arange(c)] = REC_LEVELS
    return np.concatenate(mats, axis=0).astype(np.float32), level


def _coarse_level_operand(q, k, cum, l):
    n, half = _level_split(l)
    pieces = []
    for lo in range(0, REC_CHUNK, n):
        mid, hi = lo + half, lo + n
        ref = cum[mid - 1:mid]
        pieces.append(k[lo:mid] * jnp.exp2(ref - cum[lo:mid]))
        pieces.append(q[mid:hi] * jnp.exp2(cum[mid:hi] - ref))
    return jnp.concatenate(pieces, axis=0)


def _rec_chunk(proj_ref, rows, lb_all, gain_ref, sums_ref, lvl_ref, o_ref, st_ref, heads, after_head=None):
    c = REC_CHUNK
    lvl = lvl_ref[...]
    in_level = [lvl == l for l in range(REC_LEVELS + 1)]
    rowi = lax.broadcasted_iota(jnp.int32, (c, HEAD_DIM), 0)
    second_half = {l: (rowi & _level_split(l)[1]) != 0 for l in REC_FINE_LEVELS + (REC_LEVELS - 1,)}
    sums = sums_ref[...]
    hook = iter(range(3 * heads + 1))

    def fill():
        if after_head is not None:
            after_head(next(hook))

    fill()

    gates, parts = [], []
    for j in range(heads):
        lanes = slice(j * HEAD_DIM, (j + 1) * HEAD_DIM)
        lb = lb_all[:, lanes]
        fgt = lb + (1.0 - lb) * _sigmoid(proj_ref[1, rows, lanes])
        logf = jnp.log(fgt) * LOG2_E
        hi = logf.astype(BF16)
        lo = (logf - hi.astype(F32)).astype(BF16)
        gates.append(fgt)
        parts.append(_dot(sums, jnp.concatenate([hi, lo], axis=1)))
        fill()

    all_scores, cums = [], []
    for j in range(heads):
        lanes = slice(j * HEAD_DIM, (j + 1) * HEAD_DIM)
        q = proj_ref[0, rows, lanes]
        fgt = gates[j]
        k = 1.0 - fgt
        ex = parts[j][:, :HEAD_DIM] + parts[j][:, HEAD_DIM:]
        cum = ex[:c]
        scores = jnp.zeros((c, c), F32)
        for l in range(REC_LEVELS):
            if l in REC_FINE_LEVELS:
                i = 1 + REC_FINE_LEVELS.index(l)
                xl = jnp.where(second_half[l], q, k) * jnp.exp2(ex[i * c:(i + 1) * c])
            elif l == REC_LEVELS - 1:
                xl = jnp.where(second_half[l], q * fgt, k)
            else:
                xl = _coarse_level_operand(q, k, cum, l)
            xt = xl.T.astype(BF16)
            xl = xl.astype(BF16)
            scores = jnp.where(in_level[l], _dot(xl, xt), scores)
        scores = jnp.where(in_level[REC_LEVELS], jnp.sum(q * k, axis=-1, keepdims=True), scores)
        all_scores.append(scores.astype(BF16))
        cums.append(cum)
        fill()

    for j in range(heads):
        lanes = slice(j * HEAD_DIM, (j + 1) * HEAD_DIM)
        q = proj_ref[0, rows, lanes]
        v = proj_ref[2, rows, lanes].astype(BF16)
        gate = proj_ref[3, rows, lanes]
        k = 1.0 - gates[j]
        cum = cums[j]
        st = st_ref[j]
        end = cum[c - 1:c]
        qd = (q * jnp.exp2(cum)).astype(BF16)
        o = _dot(all_scores[j], v) + _dot(qd, st.T.astype(BF16))
        kd = (k * jnp.exp2(end - cum)).astype(BF16)
        st_ref[j] = st * jnp.exp2(end) + _dot_tn(v, kd)

        ms = jnp.mean(o * o, axis=-1, keepdims=True)
        on = o * lax.rsqrt(ms + EPS) * gain_ref[:, lanes] * (gate * _sigmoid(gate))
        o_ref[rows, lanes] = on.astype(BF16)
        fill()


def _hgrn_kernel(h0_ref, hn_ref, w_ref, lbp_ref, gain_ref, sums_ref, lvl_ref, wi32_ref, wo32_ref,
                 o_ref, wi16_ref, wo16_ref, pa_ref, pb_ref, st_ref, *, layer, heads, steps_per_seq):
    n = pl.program_id(0)
    wi16_ref[...] = wi32_ref[...].astype(BF16)
    wo16_ref[...] = wo32_ref[...].astype(BF16)

    @pl.when(n == 0)
    def _():
        for part in range(N_PROJ):
            pa_ref[part] = _dot(h0_ref[...], w_ref[part])

    @pl.when(n % steps_per_seq == 0)
    def _():
        st_ref[...] = jnp.zeros_like(st_ref)

    lbp = lbp_ref[...]
    e = jnp.exp(lbp - jnp.max(lbp, axis=0, keepdims=True))
    p = e / jnp.sum(e, axis=0, keepdims=True)
    lb_all = jnp.sum(p[1:layer + 1], axis=0, keepdims=True) if layer > 0 else jnp.zeros_like(p[0:1])

    wl = heads * HEAD_DIM
    col_w = min(MXU_WIDTH, wl)
    pieces = [(r, cb) for r in range(N_PROJ) for cb in range(wl // col_w)]
    assert len(pieces) <= 3 * heads
    first = min(heads + 1, 3 * heads + 1 - len(pieces))
    fill_hooks = list(range(first, first + len(pieces)))

    def run(cur_ref, nxt_ref):
        def chunk(s, carry):
            def project_slice(hook_idx):
                if hook_idx not in fill_hooks:
                    return
                r, cb = pieces[fill_hooks.index(hook_idx)]
                rws = slice(r * REC_CHUNK, (r + 1) * REC_CHUNK)
                cols = slice(cb * col_w, (cb + 1) * col_w)
                nxt_ref[s, rws, cols] = _dot(hn_ref[rws, :], w_ref[s, :, cols])
            rows = pl.ds(pl.multiple_of(s * REC_CHUNK, REC_CHUNK), REC_CHUNK)
            _rec_chunk(cur_ref, rows, lb_all, gain_ref, sums_ref, lvl_ref, o_ref, st_ref, heads,
                       after_head=project_slice)
            return carry
        lax.fori_loop(0, N_PROJ, chunk, 0)

    @pl.when(n % 2 == 0)
    def _():
        run(pa_ref, pb_ref)

    @pl.when(n % 2 == 1)
    def _():
        run(pb_ref, pa_ref)


def _hgrn_mixer(h, w_in, hgrn_lb, out_norm_gain_l, w_ffn_in, w_ffn_out, *, batch, seq, layer):
    m, d = h.shape
    depth = hgrn_lb.shape[0]
    groups, wl = w_in.shape[0], w_in.shape[3]
    hb = wl // HEAD_DIM
    rb = N_PROJ * REC_CHUNK
    assert seq % rb == 0
    steps_per_seq = seq // rb
    steps_per_group = batch * steps_per_seq
    n_steps = groups * steps_per_group
    sums_np, lvl_np = _rec_constants()
    sums = jnp.asarray(sums_np, BF16)
    lvl = jnp.asarray(lvl_np)
    nxt = lambda n: jnp.minimum(n + 1, n_steps - 1)

    def slab_rows(total):
        hold = 1
        while n_steps % hold or total % (n_steps // hold) or (total // (n_steps // hold)) % 16:
            hold *= 2
            assert hold <= n_steps, "no slab split"
        return total // (n_steps // hold), hold

    d_ff = w_ffn_out.shape[1]
    (ri, hi), (ro, ho) = slab_rows(d), slab_rows(d_ff)

    return pl.pallas_call(
        functools.partial(_hgrn_kernel, layer=layer, heads=hb, steps_per_seq=steps_per_seq),
        grid=(n_steps,),
        in_specs=[
            pl.BlockSpec((rb, d), lambda n: (0, 0)),
            pl.BlockSpec((rb, d), lambda n: (nxt(n) % steps_per_group, 0)),
            pl.BlockSpec((None, N_PROJ, d, wl), lambda n: (nxt(n) // steps_per_group, 0, 0, 0),
                         pipeline_mode=pl.Buffered(1)),
            pl.BlockSpec((depth, wl), lambda n: (0, n // steps_per_group)),
            pl.BlockSpec((1, wl), lambda n: (0, n // steps_per_group)),
            pl.BlockSpec(sums.shape, lambda n: (0, 0)),
            pl.BlockSpec(lvl.shape, lambda n: (0, 0)),
            pl.BlockSpec((None, ri, 2 * d_ff), lambda n: (layer, n // hi, 0)),
            pl.BlockSpec((None, ro, d), lambda n: (layer, n // ho, 0)),
        ],
        out_specs=[pl.BlockSpec((rb, wl), lambda n: (n % steps_per_group, n // steps_per_group)),
                   pl.BlockSpec((ri, 2 * d_ff), lambda n: (n // hi, 0)),
                   pl.BlockSpec((ro, d), lambda n: (n // ho, 0))],
        out_shape=[jax.ShapeDtypeStruct((m, d), BF16),
                   jax.ShapeDtypeStruct((d, 2 * d_ff), BF16), jax.ShapeDtypeStruct((d_ff, d), BF16)],
        scratch_shapes=[pltpu.VMEM((N_PROJ, rb, wl), F32), pltpu.VMEM((N_PROJ, rb, wl), F32),
                        pltpu.VMEM((hb, HEAD_DIM, HEAD_DIM), F32)],
        compiler_params=pltpu.CompilerParams(
            dimension_semantics=("arbitrary",),
            vmem_limit_bytes=V7X_VMEM_LIMIT),
        name="hgrn_proj_recurrence",
    )(h, h, w_in, hgrn_lb, out_norm_gain_l.reshape(1, d), sums, lvl, w_ffn_in, w_ffn_out)


def kernel(x, c, norm_mix_gain, norm_ffn_gain, w_ada, b_ada, pool_w, pool_scale, hgrn_w_in, hgrn_w_out,
           hgrn_norm_gain, hgrn_lb, w_ffn_in, w_ffn_out, final_gain):
    batch, seq, d = x.shape
    depth = w_ada.shape[0]
    mods = _modulation(c, w_ada, b_ada).reshape(depth, batch, N_MOD, d)
    x2 = x.reshape(batch * seq, d)
    w_in16, w_out16 = _cast_ffn_weights(w_ffn_in, w_ffn_out, 0)
    pool_w16, hgrn_out16 = pool_w.astype(BF16), hgrn_w_out.astype(BF16)
    group_width = min(REC_HEADS, d // HEAD_DIM) * HEAD_DIM
    assert depth % 2 == 0, "layers alternate pooling / HGRN2; each pooling layer feeds the HGRN2 layer after it"
    h = hgrn_in16 = None
    for layer in range(depth):
        j = layer // 2
        last = layer == depth - 1
        fin = final_gain if last else None
        if layer % 2 == 0:
            mixer_args = (norm_mix_gain[layer], pool_w16, pool_scale[j], j)
            x2, h, hgrn_in16 = _ffn_layer(
                x2, mods[layer], norm_ffn_gain[layer], w_in16, w_out16, seq=seq, mixer="pool",
                mixer_args=mixer_args, final_gain=fin, cast_hgrn=(hgrn_w_in, j, group_width),
                next_mixer=(mods[layer + 1], norm_mix_gain[layer + 1]))
        else:
            cast_next = None if last else (w_ffn_in, w_ffn_out, layer + 1)
            og, w_in16, w_out16 = _hgrn_mixer(h, hgrn_in16, hgrn_lb, hgrn_norm_gain[j], w_ffn_in, w_ffn_out,
                                              batch=batch, seq=seq, layer=layer)
            mixer_args = (og, hgrn_out16, j)
            outs = _ffn_layer(x2, mods[layer], norm_ffn_gain[layer], w_in16, w_out16, seq=seq, mixer="hgrn",
                              mixer_args=mixer_args, final_gain=fin, cast_next=cast_next)
            x2 = outs[0]
            if cast_next is not None:
                w_in16, w_out16 = outs[1:]
    return x2.reshape(batch, seq, d)
```

```python
import functools

import numpy as np
import jax
import jax.numpy as jnp
from jax import lax
from jax.experimental import pallas as pl
from jax.experimental.pallas import tpu as pltpu

EPS = 1e-6
LOG2_E = 1.4426950408889634
POOL_WINDOWS = (2, 4, 8, 16)
POOL_HALO = 16
HEAD_DIM = 128
N_MOD = 6
(ROW_SH_M, ROW_SC_M, ROW_G_M, ROW_SH_F, ROW_SC_F, ROW_G_F, ROW_NEXT_SH, ROW_NEXT_SC,
 ROW_MIX_GAIN, ROW_FFN_GAIN, ROW_POOL_SCALE, ROW_NEXT_GAIN, ROW_FINAL_GAIN) = range(13)
N_VEC_ROWS = 16
FFN_SUBTILES = 2
N_PROJ = 4
REC_HEADS = 8
REC_CHUNK = 128
REC_LEVELS = 7
REC_FINE_LEVELS = (4, 5)
MXU_WIDTH = 256
V7X_VMEM_LIMIT = 56 * 1024 * 1024

F32 = jnp.float32
BF16 = jnp.bfloat16


def _sigmoid(x):
    return 1.0 / (1.0 + jnp.exp(-x))


def _mod_norm(x, gain, shift, scale):
    ms = jnp.mean(x * x, axis=-1, keepdims=True)
    return x * lax.rsqrt(ms + EPS) * (gain * (1.0 + scale)) + shift


def _dot(a, b):
    return jnp.dot(a, b, preferred_element_type=F32)


def _dot_nt(a, b):
    return lax.dot_general(a, b, (((1,), (1,)), ((), ())), preferred_element_type=F32)


def _dot_tn(a, b):
    return lax.dot_general(a, b, (((0,), (0,)), ((), ())), preferred_element_type=F32)


def _mod_kernel(c_ref, w_ref, b_ref, o_ref, *, batch):
    cpad = c_ref[...]
    cond = (cpad * _sigmoid(cpad)).astype(BF16)
    res = _dot(cond, w_ref[...].astype(BF16)) + b_ref[...]
    o_ref[...] = res[:batch]


def _modulation(c, w_ada, b_ada):
    depth, d, n = w_ada.shape
    batch = c.shape[0]
    rows = 16
    cpad = jnp.zeros((rows, d), F32).at[:batch].set(c)
    tn = min(n, 1024)
    return pl.pallas_call(
        functools.partial(_mod_kernel, batch=batch),
        grid=(depth, n // tn),
        in_specs=[
            pl.BlockSpec((rows, d), lambda l, j: (0, 0)),
            pl.BlockSpec((None, d, tn), lambda l, j: (l, 0, j)),
            pl.BlockSpec((None, 1, tn), lambda l, j: (l, 0, j)),
        ],
        out_specs=pl.BlockSpec((None, batch, tn), lambda l, j: (l, 0, j)),
        out_shape=jax.ShapeDtypeStruct((depth, batch, n), F32),
        compiler_params=pltpu.CompilerParams(
            dimension_semantics=("parallel", "parallel"),
            vmem_limit_bytes=V7X_VMEM_LIMIT),
        name="adaln_mod",
    )(cpad, w_ada, b_ada.reshape(depth, 1, n))


def _pool_mixer_rows(x, r0, xh_ref, gain, shift, scale, pw_ref, chan_scale, hs_ref, first_tile, pos_base):
    n, d = x.shape
    g_dim = d // len(POOL_WINDOWS)
    if r0 == 0:
        hh = _mod_norm(xh_ref[...], gain, shift, scale)
        hs_ref[0:POOL_HALO, :] = jnp.where(first_tile, 0.0, hh)
    hs_ref[POOL_HALO + r0:POOL_HALO + r0 + n, :] = _mod_norm(x, gain, shift, scale)
    row = lax.broadcasted_iota(jnp.int32, (POOL_HALO, g_dim), 0)
    pos = (pos_base + row + 1).astype(F32)
    ys = []
    for g, w in enumerate(POOL_WINDOWS):
        lanes = slice(g * g_dim, (g + 1) * g_dim)
        s = hs_ref[r0:POOL_HALO + r0 + n, lanes]
        span = 1
        while span < w:
            s = s + pltpu.roll(s, span, axis=0)
            span *= 2
        s = s[POOL_HALO:]
        hg = hs_ref[POOL_HALO + r0:POOL_HALO + r0 + n, lanes]
        if r0 == 0:
            top = s[:POOL_HALO] / jnp.minimum(pos, float(w)) - hg[:POOL_HALO]
            rest = s[POOL_HALO:] * (1.0 / w) - hg[POOL_HALO:]
            dg = jnp.concatenate([top, rest], axis=0)
        else:
            dg = s * (1.0 / w) - hg
        ys.append(_dot(dg.astype(BF16), pw_ref[g]))
    return jnp.concatenate(ys, axis=1) * chan_scale


def _cast_kernel(a_ref, b_ref, a16_ref, b16_ref):
    a16_ref[...] = a_ref[...].astype(BF16)
    b16_ref[...] = b_ref[...].astype(BF16)


def _cast_ffn_weights(w_in, w_out, layer, steps=16):
    _, d, n2 = w_in.shape
    _, d_ff, _ = w_out.shape
    ra, rb = d // steps, d_ff // steps
    assert ra * steps == d and rb * steps == d_ff and ra % 16 == 0 and rb % 16 == 0
    return pl.pallas_call(
        _cast_kernel,
        grid=(steps,),
        in_specs=[pl.BlockSpec((None, ra, n2), lambda s: (layer, s, 0)),
                  pl.BlockSpec((None, rb, d), lambda s: (layer, s, 0))],
        out_specs=[pl.BlockSpec((ra, n2), lambda s: (s, 0)), pl.BlockSpec((rb, d), lambda s: (s, 0))],
        out_shape=[jax.ShapeDtypeStruct((d, n2), BF16), jax.ShapeDtypeStruct((d_ff, d), BF16)],
        compiler_params=pltpu.CompilerParams(
            dimension_semantics=("parallel",), vmem_limit_bytes=V7X_VMEM_LIMIT),
        name="cast_ffn_weights",
    )(w_in, w_out)


def _ffn_kernel(*refs, mixer, final, feeds_next, n_casts, nf, tiles_per_seq, tm):
    if mixer == "pool":
        (x_ref, xh_ref, vec_ref, pw_ref, wa_ref, wb_ref, wo_ref) = refs[:7]
        rest = refs[7:]
    else:
        (x_ref, og_ref, vec_ref, wout_ref, wa_ref, wb_ref, wo_ref) = refs[:7]
        rest = refs[7:]
    vec = lambda r: vec_ref[r:r + 1, :]
    cast_src, rest = rest[:n_casts], rest[n_casts:]
    o_ref, rest = rest[0], rest[1:]
    if feeds_next:
        hn_ref, rest = rest[0], rest[1:]
    cast_dst, rest = rest[:n_casts], rest[n_casts:]
    if mixer == "pool":
        h2_ref, acc_ref, hs_ref = rest
    else:
        h2_ref, acc_ref = rest

    i = pl.program_id(0)
    f = pl.program_id(1)
    tile_in_seq = i % tiles_per_seq

    def cast_slabs():
        for src, dst in zip(cast_src, cast_dst):
            dst[...] = src[...].astype(BF16)

    def ffn_chunk(h2):
        n = h2.shape[0]
        halves = [h2] if n < 2 * MXU_WIDTH else [h2[:n // 2], h2[n // 2:]]
        us = []
        for hh in halves:
            a = _dot(hh, wa_ref[...])
            b = _dot(hh, wb_ref[...])
            us.append((a * _sigmoid(a) * b).astype(BF16))
        return jnp.concatenate([_dot(u, wo_ref[...]) for u in us], axis=0)

    @pl.when(f == 0)
    def _():
        cast_slabs()
        sub = tm // FFN_SUBTILES
        if mixer == "hgrn":
            ys = [_dot(og_ref[r0:r0 + sub, :], wout_ref[...]) for r0 in range(0, tm, sub)]
        for t, r0 in enumerate(range(0, tm, sub)):
            rows = slice(r0, r0 + sub)
            x = x_ref[rows, :]
            if mixer == "pool":
                y = _pool_mixer_rows(x, r0, xh_ref, vec(ROW_MIX_GAIN), vec(ROW_SH_M), vec(ROW_SC_M),
                                     pw_ref, vec(ROW_POOL_SCALE), hs_ref, tile_in_seq == 0, tile_in_seq * tm)
            else:
                y = ys[t]
            xm = x + vec(ROW_G_M) * y
            o_ref[rows, :] = xm
            h2 = _mod_norm(xm, vec(ROW_FFN_GAIN), vec(ROW_SH_F), vec(ROW_SC_F)).astype(BF16)
            h2_ref[rows, :] = h2
            acc_ref[rows, :] = ffn_chunk(h2)

    @pl.when(jnp.logical_and(f > 0, f < nf - 1))
    def _():
        cast_slabs()
        acc_ref[...] += ffn_chunk(h2_ref[...])

    @pl.when(f == nf - 1)
    def _():
        cast_slabs()
        sub = tm // FFN_SUBTILES
        for r0 in range(0, tm, sub):
            rows = slice(r0, r0 + sub)
            acc = acc_ref[rows, :] + ffn_chunk(h2_ref[rows, :])
            out = o_ref[rows, :] + vec(ROW_G_F) * acc
            if final:
                ms = jnp.mean(out * out, axis=-1, keepdims=True)
                out = out * lax.rsqrt(ms + EPS) * vec(ROW_FINAL_GAIN)
            o_ref[rows, :] = out
            if feeds_next:
                hn = _mod_norm(out, vec(ROW_NEXT_GAIN), vec(ROW_NEXT_SH), vec(ROW_NEXT_SC))
                hn_ref[rows, :] = hn.astype(BF16)


def _ffn_layer(x2, mod_l, norm_ffn_gain_l, w_in, w_out, *, seq, mixer, mixer_args, final_gain=None,
               next_mixer=None, cast_next=None, cast_hgrn=None, tm=512, tf=512):
    m, d = x2.shape
    d_ff = w_out.shape[0]
    tm = min(tm, seq)
    tf = min(tf, d_ff)
    assert seq % tm == 0 and d_ff % tf == 0 and tm % (FFN_SUBTILES * POOL_HALO) == 0
    nf = d_ff // tf
    assert nf >= 2, "the first and the last d_ff step are distinct code paths"
    tiles_per_seq = seq // tm
    batch = mod_l.shape[0]
    x_spec = pl.BlockSpec((tm, d), lambda i, f: (i, 0))
    shared = {ROW_FFN_GAIN: norm_ffn_gain_l}
    if mixer == "pool":
        shared[ROW_MIX_GAIN], shared[ROW_POOL_SCALE] = mixer_args[0], mixer_args[2]
    if final_gain is not None:
        shared[ROW_FINAL_GAIN] = final_gain
    if next_mixer is not None:
        shared[ROW_NEXT_GAIN] = next_mixer[1]
    next_rows = jnp.zeros((batch, 2, d), F32) if next_mixer is None else next_mixer[0][:, ROW_SH_M:ROW_SC_M + 1]
    tail = jnp.stack([shared.get(r, jnp.zeros((d,), F32)) for r in range(ROW_MIX_GAIN, N_VEC_ROWS)])
    vecs = jnp.concatenate([mod_l, next_rows, jnp.broadcast_to(tail, (batch,) + tail.shape)], axis=1)
    vec_spec = pl.BlockSpec((None, N_VEC_ROWS, d), lambda i, f: (i // tiles_per_seq, 0, 0))
    ffn_specs = [
        pl.BlockSpec((d, tf), lambda i, f: (0, f)),
        pl.BlockSpec((d, tf), lambda i, f: (0, nf + f)),
        pl.BlockSpec((tf, d), lambda i, f: (f, 0)),
    ]
    scratch = [pltpu.VMEM((tm, d), BF16), pltpu.VMEM((tm, d), F32)]
    if mixer == "pool":
        norm_mix_gain_l, pool_w, pool_scale, j = mixer_args
        _, n_groups, g_dim, _ = pool_w.shape
        halo_blocks = tm // POOL_HALO
        args = [x2, x2, vecs, pool_w]
        specs = [x_spec,
                 pl.BlockSpec((POOL_HALO, d), lambda i, f: (jnp.maximum(i * halo_blocks - 1, 0), 0)),
                 vec_spec,
                 pl.BlockSpec((None, n_groups, g_dim, g_dim), lambda i, f: (j, 0, 0, 0))]
        scratch.append(pltpu.VMEM((tm + POOL_HALO, d), F32))
    else:
        og, w_mix_out, j = mixer_args
        args = [x2, og, vecs, w_mix_out]
        specs = [x_spec, pl.BlockSpec((tm, d), lambda i, f: (i, 0)), vec_spec,
                 pl.BlockSpec((None, d, d), lambda i, f: (j, 0, 0), pipeline_mode=pl.Buffered(1))]
    args += [w_in, w_in, w_out]
    specs += ffn_specs
    out_specs = [pl.BlockSpec((tm, d), lambda i, f: (i, 0))]
    out_shape = [jax.ShapeDtypeStruct((m, d), F32)]
    if next_mixer is not None:
        out_specs.append(pl.BlockSpec((tm, d), lambda i, f: (i, 0)))
        out_shape.append(jax.ShapeDtypeStruct((m, d), BF16))
    n_i = m // tm
    cast_in, cast_in_specs, cast_out_specs, cast_out_shape = [], [], [], []
    if cast_next is not None:
        w_in32, w_out32, nxt = cast_next
        ri, ci, ro = d // n_i, 2 * d_ff // nf, d_ff // (n_i * nf)
        assert ri * n_i == d and ci * nf == 2 * d_ff and ro * n_i * nf == d_ff
        assert ri % 16 == 0 and ro % 16 == 0 and ci % 128 == 0
        cast_in += [w_in32, w_out32]
        cast_in_specs += [pl.BlockSpec((None, ri, ci), lambda i, f: (nxt, i, f)),
                          pl.BlockSpec((None, ro, d), lambda i, f: (nxt, i * nf + f, 0))]
        cast_out_specs += [pl.BlockSpec((ri, ci), lambda i, f: (i, f)),
                           pl.BlockSpec((ro, d), lambda i, f: (i * nf + f, 0))]
        cast_out_shape += [jax.ShapeDtypeStruct((d, 2 * d_ff), BF16), jax.ShapeDtypeStruct((d_ff, d), BF16)]
    if cast_hgrn is not None:
        hw32, jn, wl = cast_hgrn
        ncb = hw32.shape[2] // wl
        groups = ncb // N_PROJ
        row_blocks = max(rbk for rbk in (1, 2, 4, 8, 16, 32, 64, 128)
                         if rbk * ncb <= n_i * nf and d % rbk == 0 and (d // rbk) % 16 == 0)
        rh = d // row_blocks
        slab = lambda i, f: jnp.minimum(i * nf + f, row_blocks * ncb - 1)
        cast_in.append(hw32)
        cast_in_specs.append(pl.BlockSpec((None, rh, wl), lambda i, f: (jn, slab(i, f) // ncb, slab(i, f) % ncb)))
        cast_out_specs.append(pl.BlockSpec(
            (None, None, rh, wl),
            lambda i, f: ((slab(i, f) % ncb) % groups, (slab(i, f) % ncb) // groups, slab(i, f) // ncb, 0)))
        cast_out_shape.append(jax.ShapeDtypeStruct((groups, N_PROJ, d, wl), BF16))
    args += cast_in
    specs += cast_in_specs
    out_specs += cast_out_specs
    out_shape += cast_out_shape
    return pl.pallas_call(
        functools.partial(_ffn_kernel, mixer=mixer, final=final_gain is not None,
                          feeds_next=next_mixer is not None, n_casts=len(cast_in),
                          nf=nf, tiles_per_seq=tiles_per_seq, tm=tm),
        grid=(m // tm, nf),
        in_specs=specs,
        out_specs=out_specs,
        out_shape=out_shape,
        scratch_shapes=scratch,
        compiler_params=pltpu.CompilerParams(
            dimension_semantics=("arbitrary", "arbitrary"),
            vmem_limit_bytes=V7X_VMEM_LIMIT),
        name="mixer_out_ffn_" + mixer,
    )(*args)


def _level_split(l):
    n = REC_CHUNK >> l
    return n, n // 2


def _rec_constants():
    c = REC_CHUNK
    t = np.arange(c)[:, None]
    u = np.arange(c)[None, :]
    mats = [u <= t]
    level = np.full((c, c), -1, np.int32)
    for l in range(REC_LEVELS):
        n, half = _level_split(l)
        mid = (t // n) * n + half
        if l in REC_FINE_LEVELS:
            mats.append(np.where(t >= mid, (u >= mid) & (u <= t), (u > t) & (u < mid)))
        same = (t // n) == (u // n)
        level[same & (t % n >= half) & (u % n < half)] = l
    level[np.arange(c), np.arange(c)] = REC_LEVELS
    return np.concatenate(mats, axis=0).astype(np.float32), level


def _coarse_level_operand(q, k, cum, l):
    n, half = _level_split(l)
    pieces = []
    for lo in range(0, REC_CHUNK, n):
        mid, hi = lo + half, lo + n
        ref = cum[mid - 1:mid]
        pieces.append(k[lo:mid] * jnp.exp2(ref - cum[lo:mid]))
        pieces.append(q[mid:hi] * jnp.exp2(cum[mid:hi] - ref))
    return jnp.concatenate(pieces, axis=0)


def _rec_chunk(proj_ref, rows, lb_all, gain_ref, sums_ref, lvl_ref, o_ref, st_ref, heads, after_head=None):
    c = REC_CHUNK
    lvl = lvl_ref[...]
    in_level = [lvl == l for l in range(REC_LEVELS + 1)]
    rowi = lax.broadcasted_iota(jnp.int32, (c, HEAD_DIM), 0)
    second_half = {l: (rowi & _level_split(l)[1]) != 0 for l in REC_FINE_LEVELS + (REC_LEVELS - 1,)}
    sums = sums_ref[...]
    hook = iter(range(3 * heads + 1))

    def fill():
        if after_head is not None:
            after_head(next(hook))

    fill()

    gates, parts = [], []
    for j in range(heads):
        lanes = slice(j * HEAD_DIM, (j + 1) * HEAD_DIM)
        lb = lb_all[:, lanes]
        fgt = lb + (1.0 - lb) * _sigmoid(proj_ref[1, rows, lanes])
        logf = jnp.log(fgt) * LOG2_E
        hi = logf.astype(BF16)
        lo = (logf - hi.astype(F32)).astype(BF16)
        gates.append(fgt)
        parts.append(_dot(sums, jnp.concatenate([hi, lo], axis=1)))
        fill()

    all_scores, cums = [], []
    for j in range(heads):
        lanes = slice(j * HEAD_DIM, (j + 1) * HEAD_DIM)
        q = proj_ref[0, rows, lanes]
        fgt = gates[j]
        k = 1.0 - fgt
        ex = parts[j][:, :HEAD_DIM] + parts[j][:, HEAD_DIM:]
        cum = ex[:c]
        scores = jnp.zeros((c, c), F32)
        for l in range(REC_LEVELS):
            if l in REC_FINE_LEVELS:
                i = 1 + REC_FINE_LEVELS.index(l)
                xl = jnp.where(second_half[l], q, k) * jnp.exp2(ex[i * c:(i + 1) * c])
            elif l == REC_LEVELS - 1:
                xl = jnp.where(second_half[l], q * fgt, k)
            else:
                xl = _coarse_level_operand(q, k, cum, l)
            xt = xl.T.astype(BF16)
            xl = xl.astype(BF16)
            scores = jnp.where(in_level[l], _dot(xl, xt), scores)
        scores = jnp.where(in_level[REC_LEVELS], jnp.sum(q * k, axis=-1, keepdims=True), scores)
        all_scores.append(scores.astype(BF16))
        cums.append(cum)
        fill()

    for j in range(heads):
        lanes = slice(j * HEAD_DIM, (j + 1) * HEAD_DIM)
        q = proj_ref[0, rows, lanes]
        v = proj_ref[2, rows, lanes].astype(BF16)
        gate = proj_ref[3, rows, lanes]
        k = 1.0 - gates[j]
        cum = cums[j]
        st = st_ref[j]
        end = cum[c - 1:c]
        qd = (q * jnp.exp2(cum)).astype(BF16)
        o = _dot(all_scores[j], v) + _dot(qd, st.T.astype(BF16))
        kd = (k * jnp.exp2(end - cum)).astype(BF16)
        st_ref[j] = st * jnp.exp2(end) + _dot_tn(v, kd)

        ms = jnp.mean(o * o, axis=-1, keepdims=True)
        on = o * lax.rsqrt(ms + EPS) * gain_ref[:, lanes] * (gate * _sigmoid(gate))
        o_ref[rows, lanes] = on.astype(BF16)
        fill()


def _hgrn_kernel(h0_ref, hn_ref, w_ref, lbp_ref, gain_ref, sums_ref, lvl_ref, wi32_ref, wo32_ref,
                 o_ref, wi16_ref, wo16_ref, pa_ref, pb_ref, st_ref, *, layer, heads, steps_per_seq):
    n = pl.program_id(0)
    wi16_ref[...] = wi32_ref[...].astype(BF16)
    wo16_ref[...] = wo32_ref[...].astype(BF16)

    @pl.when(n == 0)
    def _():
        for part in range(N_PROJ):
            pa_ref[part] = _dot(h0_ref[...], w_ref[part])

    @pl.when(n % steps_per_seq == 0)
    def _():
        st_ref[...] = jnp.zeros_like(st_ref)

    lbp = lbp_ref[...]
    e = jnp.exp(lbp - jnp.max(lbp, axis=0, keepdims=True))
    p = e / jnp.sum(e, axis=0, keepdims=True)
    lb_all = jnp.sum(p[1:layer + 1], axis=0, keepdims=True) if layer > 0 else jnp.zeros_like(p[0:1])

    wl = heads * HEAD_DIM
    col_w = min(MXU_WIDTH, wl)
    pieces = [(r, cb) for r in range(N_PROJ) for cb in range(wl // col_w)]
    assert len(pieces) <= 3 * heads
    first = min(heads + 1, 3 * heads + 1 - len(pieces))
    fill_hooks = list(range(first, first + len(pieces)))

    def run(cur_ref, nxt_ref):
        def chunk(s, carry):
            def project_slice(hook_idx):
                if hook_idx not in fill_hooks:
                    return
                r, cb = pieces[fill_hooks.index(hook_idx)]
                rws = slice(r * REC_CHUNK, (r + 1) * REC_CHUNK)
                cols = slice(cb * col_w, (cb + 1) * col_w)
                nxt_ref[s, rws, cols] = _dot(hn_ref[rws, :], w_ref[s, :, cols])
            rows = pl.ds(pl.multiple_of(s * REC_CHUNK, REC_CHUNK), REC_CHUNK)
            _rec_chunk(cur_ref, rows, lb_all, gain_ref, sums_ref, lvl_ref, o_ref, st_ref, heads,
                       after_head=project_slice)
            return carry
        lax.fori_loop(0, N_PROJ, chunk, 0)

    @pl.when(n % 2 == 0)
    def _():
        run(pa_ref, pb_ref)

    @pl.when(n % 2 == 1)
    def _():
        run(pb_ref, pa_ref)


def _hgrn_mixer(h, w_in, hgrn_lb, out_norm_gain_l, w_ffn_in, w_ffn_out, *, batch, seq, layer):
    m, d = h.shape
    depth = hgrn_lb.shape[0]
    groups, wl = w_in.shape[0], w_in.shape[3]
    hb = wl // HEAD_DIM
    rb = N_PROJ * REC_CHUNK
    assert seq % rb == 0
    steps_per_seq = seq // rb
    steps_per_group = batch * steps_per_seq
    n_steps = groups * steps_per_group
    sums_np, lvl_np = _rec_constants()
    sums = jnp.asarray(sums_np, BF16)
    lvl = jnp.asarray(lvl_np)
    nxt = lambda n: jnp.minimum(n + 1, n_steps - 1)

    def slab_rows(total):
        hold = 1
        while n_steps % hold or total % (n_steps // hold) or (total // (n_steps // hold)) % 16:
            hold *= 2
            assert hold <= n_steps, "no slab split"
        return total // (n_steps // hold), hold

    d_ff = w_ffn_out.shape[1]
    (ri, hi), (ro, ho) = slab_rows(d), slab_rows(d_ff)

    return pl.pallas_call(
        functools.partial(_hgrn_kernel, layer=layer, heads=hb, steps_per_seq=steps_per_seq),
        grid=(n_steps,),
        in_specs=[
            pl.BlockSpec((rb, d), lambda n: (0, 0)),
            pl.BlockSpec((rb, d), lambda n: (nxt(n) % steps_per_group, 0)),
            pl.BlockSpec((None, N_PROJ, d, wl), lambda n: (nxt(n) // steps_per_group, 0, 0, 0),
                         pipeline_mode=pl.Buffered(1)),
            pl.BlockSpec((depth, wl), lambda n: (0, n // steps_per_group)),
            pl.BlockSpec((1, wl), lambda n: (0, n // steps_per_group)),
            pl.BlockSpec(sums.shape, lambda n: (0, 0)),
            pl.BlockSpec(lvl.shape, lambda n: (0, 0)),
            pl.BlockSpec((None, ri, 2 * d_ff), lambda n: (layer, n // hi, 0)),
            pl.BlockSpec((None, ro, d), lambda n: (layer, n // ho, 0)),
        ],
        out_specs=[pl.BlockSpec((rb, wl), lambda n: (n % steps_per_group, n // steps_per_group)),
                   pl.BlockSpec((ri, 2 * d_ff), lambda n: (n // hi, 0)),
                   pl.BlockSpec((ro, d), lambda n: (n // ho, 0))],
        out_shape=[jax.ShapeDtypeStruct((m, d), BF16),
                   jax.ShapeDtypeStruct((d, 2 * d_ff), BF16), jax.ShapeDtypeStruct((d_ff, d), BF16)],
        scratch_shapes=[pltpu.VMEM((N_PROJ, rb, wl), F32), pltpu.VMEM((N_PROJ, rb, wl), F32),
                        pltpu.VMEM((hb, HEAD_DIM, HEAD_DIM), F32)],
        compiler_params=pltpu.CompilerParams(
            dimension_semantics=("arbitrary",),
            vmem_limit_bytes=V7X_VMEM_LIMIT),
        name="hgrn_proj_recurrence",
    )(h, h, w_in, hgrn_lb, out_norm_gain_l.reshape(1, d), sums, lvl, w_ffn_in, w_ffn_out)


def kernel(x, c, norm_mix_gain, norm_ffn_gain, w_ada, b_ada, pool_w, pool_scale, hgrn_w_in, hgrn_w_out,
           hgrn_norm_gain, hgrn_lb, w_ffn_in, w_ffn_out, final_gain):
    batch, seq, d = x.shape
    depth = w_ada.shape[0]
    mods = _modulation(c, w_ada, b_ada).reshape(depth, batch, N_MOD, d)
    x2 = x.reshape(batch * seq, d)
    w_in16, w_out16 = _cast_ffn_weights(w_ffn_in, w_ffn_out, 0)
    pool_w16, hgrn_out16 = pool_w.astype(BF16), hgrn_w_out.astype(BF16)
    group_width = min(REC_HEADS, d // HEAD_DIM) * HEAD_DIM
    assert depth % 2 == 0, "layers alternate pooling / HGRN2; each pooling layer feeds the HGRN2 layer after it"
    h = hgrn_in16 = None
    for layer in range(depth):
        j = layer // 2
        last = layer == depth - 1
        fin = final_gain if last else None
        if layer % 2 == 0:
            mixer_args = (norm_mix_gain[layer], pool_w16, pool_scale[j], j)
            x2, h, hgrn_in16 = _ffn_layer(
                x2, mods[layer], norm_ffn_gain[layer], w_in16, w_out16, seq=seq, mixer="pool",
                mixer_args=mixer_args, final_gain=fin, cast_hgrn=(hgrn_w_in, j, group_width),
                next_mixer=(mods[layer + 1], norm_mix_gain[layer + 1]))
        else:
            cast_next = None if last else (w_ffn_in, w_ffn_out, layer + 1)
            og, w_in16, w_out16 = _hgrn_mixer(h, hgrn_in16, hgrn_lb, hgrn_norm_gain[j], w_ffn_in, w_ffn_out,
                                              batch=batch, seq=seq, layer=layer)
            mixer_args = (og, hgrn_out16, j)
            outs = _ffn_layer(x2, mods[layer], norm_ffn_gain[layer], w_in16, w_out16, seq=seq, mixer="hgrn",
                              mixer_args=mixer_args, final_gain=fin, cast_next=cast_next)
            x2 = outs[0]
            if cast_next is not None:
                w_in16, w_out16 = outs[1:]
    return x2.reshape(batch, seq, d)
```

```python
import functools

import numpy as np
import jax
import jax.numpy as jnp
from jax import lax
from jax.experimental import pallas as pl
from jax.experimental.pallas import tpu as pltpu

EPS = 1e-6
LOG2_E = 1.4426950408889634
POOL_WINDOWS = (2, 4, 8, 16)
POOL_HALO = 16
HEAD_DIM = 128
N_MOD = 6
(ROW_SH_M, ROW_SC_M, ROW_G_M, ROW_SH_F, ROW_SC_F, ROW_G_F, ROW_NEXT_SH, ROW_NEXT_SC,
 ROW_MIX_GAIN, ROW_FFN_GAIN, ROW_POOL_SCALE, ROW_NEXT_GAIN, ROW_FINAL_GAIN) = range(13)
N_VEC_ROWS = 16
FFN_SUBTILES = 2
N_PROJ = 4
REC_HEADS = 8
REC_CHUNK = 128
REC_LEVELS = 7
REC_FINE_LEVELS = (4, 5)
MXU_WIDTH = 256
LANES = 128
BF16_TILE_ROWS = 16
V7X_VMEM_BYTES = 64 * 1024 * 1024
V7X_VMEM_LIMIT = V7X_VMEM_BYTES - 8 * 1024 * 1024

F32 = jnp.float32
BF16 = jnp.bfloat16


def _sigmoid(x):
    return 1.0 / (1.0 + jnp.exp(-x))


def _mod_norm(x, gain, shift, scale):
    ms = jnp.mean(x * x, axis=-1, keepdims=True)
    return x * lax.rsqrt(ms + EPS) * (gain * (1.0 + scale)) + shift


def _dot(a, b):
    return jnp.dot(a, b, preferred_element_type=F32)


def _dot_tn(a, b):
    return lax.dot_general(a, b, (((0,), (0,)), ((), ())), preferred_element_type=F32)


def _mod_kernel(c_ref, w_ref, b_ref, o_ref, *, batch):
    cpad = c_ref[...]
    cond = (cpad * _sigmoid(cpad)).astype(BF16)
    res = _dot(cond, w_ref[...].astype(BF16)) + b_ref[...]
    o_ref[...] = res[:batch]


def _modulation(c, w_ada, b_ada):
    depth, d, n = w_ada.shape
    batch = c.shape[0]
    rows = BF16_TILE_ROWS
    cpad = jnp.zeros((rows, d), F32).at[:batch].set(c)
    tn = min(n, 8 * LANES)
    return pl.pallas_call(
        functools.partial(_mod_kernel, batch=batch),
        grid=(depth, n // tn),
        in_specs=[
            pl.BlockSpec((rows, d), lambda l, j: (0, 0)),
            pl.BlockSpec((None, d, tn), lambda l, j: (l, 0, j)),
            pl.BlockSpec((None, 1, tn), lambda l, j: (l, 0, j)),
        ],
        out_specs=pl.BlockSpec((None, batch, tn), lambda l, j: (l, 0, j)),
        out_shape=jax.ShapeDtypeStruct((depth, batch, n), F32),
        compiler_params=pltpu.CompilerParams(
            dimension_semantics=("parallel", "parallel"),
            vmem_limit_bytes=V7X_VMEM_LIMIT),
        name="adaln_mod",
    )(cpad, w_ada, b_ada.reshape(depth, 1, n))


def _pool_mixer_rows(x, r0, xh_ref, gain, shift, scale, pw_ref, chan_scale, hs_ref, first_tile, pos_base):
    n, d = x.shape
    g_dim = d // len(POOL_WINDOWS)
    if r0 == 0:
        hh = _mod_norm(xh_ref[...], gain, shift, scale)
        hs_ref[0:POOL_HALO, :] = jnp.where(first_tile, 0.0, hh)
    hs_ref[POOL_HALO + r0:POOL_HALO + r0 + n, :] = _mod_norm(x, gain, shift, scale)
    row = lax.broadcasted_iota(jnp.int32, (POOL_HALO, g_dim), 0)
    pos = (pos_base + row + 1).astype(F32)
    ys = []
    for g, w in enumerate(POOL_WINDOWS):
        lanes = slice(g * g_dim, (g + 1) * g_dim)
        s = hs_ref[r0:POOL_HALO + r0 + n, lanes]
        span = 1
        while span < w:
            s = s + pltpu.roll(s, span, axis=0)
            span *= 2
        s = s[POOL_HALO:]
        hg = hs_ref[POOL_HALO + r0:POOL_HALO + r0 + n, lanes]
        if r0 == 0:
            top = s[:POOL_HALO] / jnp.minimum(pos, float(w)) - hg[:POOL_HALO]
            rest = s[POOL_HALO:] * (1.0 / w) - hg[POOL_HALO:]
            dg = jnp.concatenate([top, rest], axis=0)
        else:
            dg = s * (1.0 / w) - hg
        ys.append(_dot(dg.astype(BF16), pw_ref[g]))
    return jnp.concatenate(ys, axis=1) * chan_scale


def _cast_kernel(a_ref, b_ref, a16_ref, b16_ref):
    a16_ref[...] = a_ref[...].astype(BF16)
    b16_ref[...] = b_ref[...].astype(BF16)


def _cast_ffn_weights(w_in, w_out, layer, steps=16):
    _, d, n2 = w_in.shape
    _, d_ff, _ = w_out.shape
    ra, rb = d // steps, d_ff // steps
    assert ra * steps == d and rb * steps == d_ff and ra % BF16_TILE_ROWS == 0 and rb % BF16_TILE_ROWS == 0
    return pl.pallas_call(
        _cast_kernel,
        grid=(steps,),
        in_specs=[pl.BlockSpec((None, ra, n2), lambda s: (layer, s, 0)),
                  pl.BlockSpec((None, rb, d), lambda s: (layer, s, 0))],
        out_specs=[pl.BlockSpec((ra, n2), lambda s: (s, 0)), pl.BlockSpec((rb, d), lambda s: (s, 0))],
        out_shape=[jax.ShapeDtypeStruct((d, n2), BF16), jax.ShapeDtypeStruct((d_ff, d), BF16)],
        compiler_params=pltpu.CompilerParams(
            dimension_semantics=("parallel",), vmem_limit_bytes=V7X_VMEM_LIMIT),
        name="cast_ffn_weights",
    )(w_in, w_out)


def _ffn_kernel(*refs, mixer, final, feeds_next, n_casts, nf, tiles_per_seq, tm):
    if mixer == "pool":
        (x_ref, xh_ref, vec_ref, pw_ref, wa_ref, wb_ref, wo_ref) = refs[:7]
        rest = refs[7:]
    else:
        (x_ref, og_ref, vec_ref, wout_ref, wa_ref, wb_ref, wo_ref) = refs[:7]
        rest = refs[7:]
    vec = lambda r: vec_ref[r:r + 1, :]
    cast_src, rest = rest[:n_casts], rest[n_casts:]
    o_ref, rest = rest[0], rest[1:]
    if feeds_next:
        hn_ref, rest = rest[0], rest[1:]
    cast_dst, rest = rest[:n_casts], rest[n_casts:]
    if mixer == "pool":
        h2_ref, acc_ref, hs_ref = rest
    else:
        h2_ref, acc_ref = rest

    i = pl.program_id(0)
    f = pl.program_id(1)
    tile_in_seq = i % tiles_per_seq

    def cast_slabs():
        for src, dst in zip(cast_src, cast_dst):
            dst[...] = src[...].astype(BF16)

    def ffn_chunk(h2):
        n = h2.shape[0]
        halves = [h2] if n < 2 * MXU_WIDTH else [h2[:n // 2], h2[n // 2:]]
        us = []
        for hh in halves:
            a = _dot(hh, wa_ref[...])
            b = _dot(hh, wb_ref[...])
            us.append((a * _sigmoid(a) * b).astype(BF16))
        return jnp.concatenate([_dot(u, wo_ref[...]) for u in us], axis=0)

    @pl.when(f == 0)
    def _():
        cast_slabs()
        sub = tm // FFN_SUBTILES
        if mixer == "hgrn":
            ys = [_dot(og_ref[r0:r0 + sub, :], wout_ref[...]) for r0 in range(0, tm, sub)]
        for t, r0 in enumerate(range(0, tm, sub)):
            rows = slice(r0, r0 + sub)
            x = x_ref[rows, :]
            if mixer == "pool":
                y = _pool_mixer_rows(x, r0, xh_ref, vec(ROW_MIX_GAIN), vec(ROW_SH_M), vec(ROW_SC_M),
                                     pw_ref, vec(ROW_POOL_SCALE), hs_ref, tile_in_seq == 0, tile_in_seq * tm)
            else:
                y = ys[t]
            xm = x + vec(ROW_G_M) * y
            o_ref[rows, :] = xm
            h2 = _mod_norm(xm, vec(ROW_FFN_GAIN), vec(ROW_SH_F), vec(ROW_SC_F)).astype(BF16)
            h2_ref[rows, :] = h2
            acc_ref[rows, :] = ffn_chunk(h2)

    @pl.when(jnp.logical_and(f > 0, f < nf - 1))
    def _():
        cast_slabs()
        acc_ref[...] += ffn_chunk(h2_ref[...])

    @pl.when(f == nf - 1)
    def _():
        cast_slabs()
        sub = tm // FFN_SUBTILES
        for r0 in range(0, tm, sub):
            rows = slice(r0, r0 + sub)
            acc = acc_ref[rows, :] + ffn_chunk(h2_ref[rows, :])
            out = o_ref[rows, :] + vec(ROW_G_F) * acc
            if final:
                ms = jnp.mean(out * out, axis=-1, keepdims=True)
                out = out * lax.rsqrt(ms + EPS) * vec(ROW_FINAL_GAIN)
            o_ref[rows, :] = out
            if feeds_next:
                hn = _mod_norm(out, vec(ROW_NEXT_GAIN), vec(ROW_NEXT_SH), vec(ROW_NEXT_SC))
                hn_ref[rows, :] = hn.astype(BF16)


def _ffn_layer(x2, mod_l, norm_ffn_gain_l, w_in, w_out, *, seq, mixer, mixer_args, final_gain=None,
               next_mixer=None, cast_next=None, cast_hgrn=None, tm=512, tf=512):
    m, d = x2.shape
    d_ff = w_out.shape[0]
    tm = min(tm, seq)
    tf = min(tf, d_ff)
    assert seq % tm == 0 and d_ff % tf == 0 and tm % (FFN_SUBTILES * POOL_HALO) == 0
    nf = d_ff // tf
    assert nf >= 2, "the first and the last d_ff step are distinct code paths"
    tiles_per_seq = seq // tm
    batch = mod_l.shape[0]
    x_spec = pl.BlockSpec((tm, d), lambda i, f: (i, 0))
    shared = {ROW_FFN_GAIN: norm_ffn_gain_l}
    if mixer == "pool":
        shared[ROW_MIX_GAIN], shared[ROW_POOL_SCALE] = mixer_args[0], mixer_args[2]
    if final_gain is not None:
        shared[ROW_FINAL_GAIN] = final_gain
    if next_mixer is not None:
        shared[ROW_NEXT_GAIN] = next_mixer[1]
    next_rows = jnp.zeros((batch, 2, d), F32) if next_mixer is None else next_mixer[0][:, ROW_SH_M:ROW_SC_M + 1]
    tail = jnp.stack([shared.get(r, jnp.zeros((d,), F32)) for r in range(ROW_MIX_GAIN, N_VEC_ROWS)])
    vecs = jnp.concatenate([mod_l, next_rows, jnp.broadcast_to(tail, (batch,) + tail.shape)], axis=1)
    vec_spec = pl.BlockSpec((None, N_VEC_ROWS, d), lambda i, f: (i // tiles_per_seq, 0, 0))
    ffn_specs = [
        pl.BlockSpec((d, tf), lambda i, f: (0, f)),
        pl.BlockSpec((d, tf), lambda i, f: (0, nf + f)),
        pl.BlockSpec((tf, d), lambda i, f: (f, 0)),
    ]
    scratch = [pltpu.VMEM((tm, d), BF16), pltpu.VMEM((tm, d), F32)]
    if mixer == "pool":
        norm_mix_gain_l, pool_w, pool_scale, j = mixer_args
        _, n_groups, g_dim, _ = pool_w.shape
        halo_blocks = tm // POOL_HALO
        args = [x2, x2, vecs, pool_w]
        specs = [x_spec,
                 pl.BlockSpec((POOL_HALO, d), lambda i, f: (jnp.maximum(i * halo_blocks - 1, 0), 0)),
                 vec_spec,
                 pl.BlockSpec((None, n_groups, g_dim, g_dim), lambda i, f: (j, 0, 0, 0))]
        scratch.append(pltpu.VMEM((tm + POOL_HALO, d), F32))
    else:
        og, w_mix_out, j = mixer_args
        args = [x2, og, vecs, w_mix_out]
        specs = [x_spec, pl.BlockSpec((tm, d), lambda i, f: (i, 0)), vec_spec,
                 pl.BlockSpec((None, d, d), lambda i, f: (j, 0, 0), pipeline_mode=pl.Buffered(1))]
    args += [w_in, w_in, w_out]
    specs += ffn_specs
    out_specs = [pl.BlockSpec((tm, d), lambda i, f: (i, 0))]
    out_shape = [jax.ShapeDtypeStruct((m, d), F32)]
    if next_mixer is not None:
        out_specs.append(pl.BlockSpec((tm, d), lambda i, f: (i, 0)))
        out_shape.append(jax.ShapeDtypeStruct((m, d), BF16))
    n_i = m // tm
    cast_in, cast_in_specs, cast_out_specs, cast_out_shape = [], [], [], []
    if cast_next is not None:
        w_in32, w_out32, nxt = cast_next
        ri, ci, ro = d // n_i, 2 * d_ff // nf, d_ff // (n_i * nf)
        assert ri * n_i == d and ci * nf == 2 * d_ff and ro * n_i * nf == d_ff
        assert ri % BF16_TILE_ROWS == 0 and ro % BF16_TILE_ROWS == 0 and ci % LANES == 0
        cast_in += [w_in32, w_out32]
        cast_in_specs += [pl.BlockSpec((None, ri, ci), lambda i, f: (nxt, i, f)),
                          pl.BlockSpec((None, ro, d), lambda i, f: (nxt, i * nf + f, 0))]
        cast_out_specs += [pl.BlockSpec((ri, ci), lambda i, f: (i, f)),
                           pl.BlockSpec((ro, d), lambda i, f: (i * nf + f, 0))]
        cast_out_shape += [jax.ShapeDtypeStruct((d, 2 * d_ff), BF16), jax.ShapeDtypeStruct((d_ff, d), BF16)]
    if cast_hgrn is not None:
        hw32, jn, wl = cast_hgrn
        ncb = hw32.shape[2] // wl
        groups = ncb // N_PROJ
        row_blocks = max(rbk for rbk in (1, 2, 4, 8, 16, 32, 64, 128)
                         if rbk * ncb <= n_i * nf and d % rbk == 0 and (d // rbk) % BF16_TILE_ROWS == 0)
        rh = d // row_blocks
        slab = lambda i, f: jnp.minimum(i * nf + f, row_blocks * ncb - 1)
        cast_in.append(hw32)
        cast_in_specs.append(pl.BlockSpec((None, rh, wl), lambda i, f: (jn, slab(i, f) // ncb, slab(i, f) % ncb)))
        cast_out_specs.append(pl.BlockSpec(
            (None, None, rh, wl),
            lambda i, f: ((slab(i, f) % ncb) % groups, (slab(i, f) % ncb) // groups, slab(i, f) // ncb, 0)))
        cast_out_shape.append(jax.ShapeDtypeStruct((groups, N_PROJ, d, wl), BF16))
    args += cast_in
    specs += cast_in_specs
    out_specs += cast_out_specs
    out_shape += cast_out_shape
    return pl.pallas_call(
        functools.partial(_ffn_kernel, mixer=mixer, final=final_gain is not None,
                          feeds_next=next_mixer is not None, n_casts=len(cast_in),
                          nf=nf, tiles_per_seq=tiles_per_seq, tm=tm),
        grid=(m // tm, nf),
        in_specs=specs,
        out_specs=out_specs,
        out_shape=out_shape,
        scratch_shapes=scratch,
        compiler_params=pltpu.CompilerParams(
            dimension_semantics=("arbitrary", "arbitrary"),
            vmem_limit_bytes=V7X_VMEM_LIMIT),
        name="mixer_out_ffn_" + mixer,
    )(*args)


def _level_split(l):
    n = REC_CHUNK >> l
    return n, n // 2


def _rec_constants():
    c = REC_CHUNK
    t = np.arange(c)[:, None]
    u = np.arange(c)[None, :]
    mats = [u <= t]
    level = np.full((c, c), -1, np.int32)
    for l in range(REC_LEVELS):
        n, half = _level_split(l)
        mid = (t // n) * n + half
        if l in REC_FINE_LEVELS:
            mats.append(np.where(t >= mid, (u >= mid) & (u <= t), (u > t) & (u < mid)))
        same = (t // n) == (u // n)
        level[same & (t % n >= half) & (u % n < half)] = l
    level[np.arange(c), np.arange(c)] = REC_LEVELS
    return np.concatenate(mats, axis=0).astype(np.float32), level


def _coarse_level_operand(q, k, cum, l):
    n, half = _level_split(l)
    pieces = []
    for lo in range(0, REC_CHUNK, n):
        mid, hi = lo + half, lo + n
        ref = cum[mid - 1:mid]
        pieces.append(k[lo:mid] * jnp.exp2(ref - cum[lo:mid]))
        pieces.append(q[mid:hi] * jnp.exp2(cum[mid:hi] - ref))
    return jnp.concatenate(pieces, axis=0)


def _rec_chunk(proj_ref, rows, lb_all, gain_ref, sums_ref, lvl_ref, o_ref, st_ref, heads, after_head=None):
    c = REC_CHUNK
    lvl = lvl_ref[...]
    in_level = [lvl == l for l in range(REC_LEVELS + 1)]
    rowi = lax.broadcasted_iota(jnp.int32, (c, HEAD_DIM), 0)
    second_half = {l: (rowi & _level_split(l)[1]) != 0 for l in REC_FINE_LEVELS + (REC_LEVELS - 1,)}
    sums = sums_ref[...]
    hook = iter(range(3 * heads + 1))

    def fill():
        if after_head is not None:
            after_head(next(hook))

    fill()

    gates, parts = [], []
    for j in range(heads):
        lanes = slice(j * HEAD_DIM, (j + 1) * HEAD_DIM)
        lb = lb_all[:, lanes]
        fgt = lb + (1.0 - lb) * _sigmoid(proj_ref[1, rows, lanes])
        logf = jnp.log(fgt) * LOG2_E
        hi = logf.astype(BF16)
        lo = (logf - hi.astype(F32)).astype(BF16)
        gates.append(fgt)
        parts.append(_dot(sums, jnp.concatenate([hi, lo], axis=1)))
        fill()

    all_scores, cums = [], []
    for j in range(heads):
        lanes = slice(j * HEAD_DIM, (j + 1) * HEAD_DIM)
        q = proj_ref[0, rows, lanes]
        fgt = gates[j]
        k = 1.0 - fgt
        ex = parts[j][:, :HEAD_DIM] + parts[j][:, HEAD_DIM:]
        cum = ex[:c]
        scores = jnp.zeros((c, c), F32)
        for l in range(REC_LEVELS):
            if l in REC_FINE_LEVELS:
                i = 1 + REC_FINE_LEVELS.index(l)
                xl = jnp.where(second_half[l], q, k) * jnp.exp2(ex[i * c:(i + 1) * c])
            elif l == REC_LEVELS - 1:
                xl = jnp.where(second_half[l], q * fgt, k)
            else:
                xl = _coarse_level_operand(q, k, cum, l)
            xt = xl.T.astype(BF16)
            xl = xl.astype(BF16)
            scores = jnp.where(in_level[l], _dot(xl, xt), scores)
        scores = jnp.where(in_level[REC_LEVELS], jnp.sum(q * k, axis=-1, keepdims=True), scores)
        all_scores.append(scores.astype(BF16))
        cums.append(cum)
        fill()

    for j in range(heads):
        lanes = slice(j * HEAD_DIM, (j + 1) * HEAD_DIM)
        q = proj_ref[0, rows, lanes]
        v = proj_ref[2, rows, lanes].astype(BF16)
        gate = proj_ref[3, rows, lanes]
        k = 1.0 - gates[j]
        cum = cums[j]
        st = st_ref[j]
        end = cum[c - 1:c]
        qd = (q * jnp.exp2(cum)).astype(BF16)
        o = _dot(all_scores[j], v) + _dot(qd, st.T.astype(BF16))
        kd = (k * jnp.exp2(end - cum)).astype(BF16)
        st_ref[j] = st * jnp.exp2(end) + _dot_tn(v, kd)

        ms = jnp.mean(o * o, axis=-1, keepdims=True)
        on = o * lax.rsqrt(ms + EPS) * gain_ref[:, lanes] * (gate * _sigmoid(gate))
        o_ref[rows, lanes] = on.astype(BF16)
        fill()


def _hgrn_kernel(h0_ref, hn_ref, w_ref, lbp_ref, gain_ref, sums_ref, lvl_ref, wi32_ref, wo32_ref,
                 o_ref, wi16_ref, wo16_ref, pa_ref, pb_ref, st_ref, *, layer, heads, steps_per_seq):
    n = pl.program_id(0)
    wi16_ref[...] = wi32_ref[...].astype(BF16)
    wo16_ref[...] = wo32_ref[...].astype(BF16)

    @pl.when(n == 0)
    def _():
        for part in range(N_PROJ):
            pa_ref[part] = _dot(h0_ref[...], w_ref[part])

    @pl.when(n % steps_per_seq == 0)
    def _():
        st_ref[...] = jnp.zeros_like(st_ref)

    lbp = lbp_ref[...]
    e = jnp.exp(lbp - jnp.max(lbp, axis=0, keepdims=True))
    p = e / jnp.sum(e, axis=0, keepdims=True)
    lb_all = jnp.sum(p[1:layer + 1], axis=0, keepdims=True) if layer > 0 else jnp.zeros_like(p[0:1])

    wl = heads * HEAD_DIM
    col_w = min(MXU_WIDTH, wl)
    pieces = [(r, cb) for r in range(N_PROJ) for cb in range(wl // col_w)]
    assert len(pieces) <= 3 * heads
    first = min(heads + 1, 3 * heads + 1 - len(pieces))
    fill_hooks = list(range(first, first + len(pieces)))

    def run(cur_ref, nxt_ref):
        def chunk(s, carry):
            def project_slice(hook_idx):
                if hook_idx not in fill_hooks:
                    return
                r, cb = pieces[fill_hooks.index(hook_idx)]
                rws = slice(r * REC_CHUNK, (r + 1) * REC_CHUNK)
                cols = slice(cb * col_w, (cb + 1) * col_w)
                nxt_ref[s, rws, cols] = _dot(hn_ref[rws, :], w_ref[s, :, cols])
            rows = pl.ds(pl.multiple_of(s * REC_CHUNK, REC_CHUNK), REC_CHUNK)
            _rec_chunk(cur_ref, rows, lb_all, gain_ref, sums_ref, lvl_ref, o_ref, st_ref, heads,
                       after_head=project_slice)
            return carry
        lax.fori_loop(0, N_PROJ, chunk, 0)

    @pl.when(n % 2 == 0)
    def _():
        run(pa_ref, pb_ref)

    @pl.when(n % 2 == 1)
    def _():
        run(pb_ref, pa_ref)


def _hgrn_mixer(h, w_in, hgrn_lb, out_norm_gain_l, w_ffn_in, w_ffn_out, *, batch, seq, layer):
    m, d = h.shape
    depth = hgrn_lb.shape[0]
    groups, wl = w_in.shape[0], w_in.shape[3]
    hb = wl // HEAD_DIM
    rb = N_PROJ * REC_CHUNK
    assert seq % rb == 0
    steps_per_seq = seq // rb
    steps_per_group = batch * steps_per_seq
    n_steps = groups * steps_per_group
    sums_np, lvl_np = _rec_constants()
    sums = jnp.asarray(sums_np, BF16)
    lvl = jnp.asarray(lvl_np)
    nxt = lambda n: jnp.minimum(n + 1, n_steps - 1)

    def slab_rows(total):
        hold = 1
        while n_steps % hold or total % (n_steps // hold) or (total // (n_steps // hold)) % BF16_TILE_ROWS:
            hold *= 2
            assert hold <= n_steps, "no slab split"
        return total // (n_steps // hold), hold

    d_ff = w_ffn_out.shape[1]
    (ri, hi), (ro, ho) = slab_rows(d), slab_rows(d_ff)

    return pl.pallas_call(
        functools.partial(_hgrn_kernel, layer=layer, heads=hb, steps_per_seq=steps_per_seq),
        grid=(n_steps,),
        in_specs=[
            pl.BlockSpec((rb, d), lambda n: (0, 0)),
            pl.BlockSpec((rb, d), lambda n: (nxt(n) % steps_per_group, 0)),
            pl.BlockSpec((None, N_PROJ, d, wl), lambda n: (nxt(n) // steps_per_group, 0, 0, 0),
                         pipeline_mode=pl.Buffered(1)),
            pl.BlockSpec((depth, wl), lambda n: (0, n // steps_per_group)),
            pl.BlockSpec((1, wl), lambda n: (0, n // steps_per_group)),
            pl.BlockSpec(sums.shape, lambda n: (0, 0)),
            pl.BlockSpec(lvl.shape, lambda n: (0, 0)),
            pl.BlockSpec((None, ri, 2 * d_ff), lambda n: (layer, n // hi, 0)),
            pl.BlockSpec((None, ro, d), lambda n: (layer, n // ho, 0)),
        ],
        out_specs=[pl.BlockSpec((rb, wl), lambda n: (n % steps_per_group, n // steps_per_group)),
                   pl.BlockSpec((ri, 2 * d_ff), lambda n: (n // hi, 0)),
                   pl.BlockSpec((ro, d), lambda n: (n // ho, 0))],
        out_shape=[jax.ShapeDtypeStruct((m, d), BF16),
                   jax.ShapeDtypeStruct((d, 2 * d_ff), BF16), jax.ShapeDtypeStruct((d_ff, d), BF16)],
        scratch_shapes=[pltpu.VMEM((N_PROJ, rb, wl), F32), pltpu.VMEM((N_PROJ, rb, wl), F32),
                        pltpu.VMEM((hb, HEAD_DIM, HEAD_DIM), F32)],
        compiler_params=pltpu.CompilerParams(
            dimension_semantics=("arbitrary",),
            vmem_limit_bytes=V7X_VMEM_LIMIT),
        name="hgrn_proj_recurrence",
    )(h, h, w_in, hgrn_lb, out_norm_gain_l.reshape(1, d), sums, lvl, w_ffn_in, w_ffn_out)


def kernel(x, c, norm_mix_gain, norm_ffn_gain, w_ada, b_ada, pool_w, pool_scale, hgrn_w_in, hgrn_w_out,
           hgrn_norm_gain, hgrn_lb, w_ffn_in, w_ffn_out, final_gain):
    batch, seq, d = x.shape
    depth = w_ada.shape[0]
    mods = _modulation(c, w_ada, b_ada).reshape(depth, batch, N_MOD, d)
    x2 = x.reshape(batch * seq, d)
    w_in16, w_out16 = _cast_ffn_weights(w_ffn_in, w_ffn_out, 0)
    pool_w16, hgrn_out16 = pool_w.astype(BF16), hgrn_w_out.astype(BF16)
    group_width = min(REC_HEADS, d // HEAD_DIM) * HEAD_DIM
    assert depth % 2 == 0, "layers alternate pooling / HGRN2; each pooling layer feeds the HGRN2 layer after it"
    h = hgrn_in16 = None
    for layer in range(depth):
        j = layer // 2
        last = layer == depth - 1
        fin = final_gain if last else None
        if layer % 2 == 0:
            mixer_args = (norm_mix_gain[layer], pool_w16, pool_scale[j], j)
            x2, h, hgrn_in16 = _ffn_layer(
                x2, mods[layer], norm_ffn_gain[layer], w_in16, w_out16, seq=seq, mixer="pool",
                mixer_args=mixer_args, final_gain=fin, cast_hgrn=(hgrn_w_in, j, group_width),
                next_mixer=(mods[layer + 1], norm_mix_gain[layer + 1]))
        else:
            cast_next = None if last else (w_ffn_in, w_ffn_out, layer + 1)
            og, w_in16, w_out16 = _hgrn_mixer(h, hgrn_in16, hgrn_lb, hgrn_norm_gain[j], w_ffn_in, w_ffn_out,
                                              batch=batch, seq=seq, layer=layer)
            mixer_args = (og, hgrn_out16, j)
            outs = _ffn_layer(x2, mods[layer], norm_ffn_gain[layer], w_in16, w_out16, seq=seq, mixer="hgrn",
                              mixer_args=mixer_args, final_gain=fin, cast_next=cast_next)
            x2 = outs[0]
            if cast_next is not None:
                w_in16, w_out16 = outs[1:]
    return x2.reshape(batch, seq, d)
```
